```python
import jax, jax.numpy as jnp
from jax import lax
import numpy as np

D_MODEL = 2048
BATCH = 2
SEQ = 8192
DEPTH = 4

N_MIXERS = 4
Q_BLOCK = 128
EPS = 1e-6
FOX_HEADS = 16
FOX_HEAD_DIM = D_MODEL // FOX_HEADS
LRU_WIDTH = 2688
LRU_BLOCKS = 16
LRU_BLOCK_DIM = LRU_WIDTH // LRU_BLOCKS
CONV_WIDTH = 4
LRU_C = 8.0
SB_HEADS = 16
SB_HEAD_DIM = D_MODEL // SB_HEADS
MLSTM_HEADS = 4
MLSTM_QK_DIM = D_MODEL // 2 // MLSTM_HEADS
MLSTM_V_DIM = D_MODEL // MLSTM_HEADS
MLSTM_CHUNK = 64
MLSTM_M_INIT = -1e30
D_FF = 4 * D_MODEL

kernel_name = 'interleaved_fox_rglru_stickbreak_mlstm_trunk'


def _n_of(m):
    return (DEPTH - m + N_MIXERS - 1) // N_MIXERS


def rmsnorm(x, g):
    xf = x.astype(jnp.float32)
    y = xf * lax.rsqrt(jnp.mean(xf * xf, axis=-1, keepdims=True) + EPS)
    return (y * g.astype(jnp.float32)).astype(x.dtype)


def _heads(t, n_heads, d):
    b, s, _ = t.shape
    return t.reshape(b, s, n_heads, d).transpose(0, 2, 1, 3)


def _blocks_to_seq(out):
    nb, b, h, q, d = out.shape
    return out.transpose(1, 0, 3, 2, 4).reshape(b, nb * q, h * d)


def fox_attention(x, w_in, b_f, w_out):
    B, S, _ = x.shape
    H, dh = FOX_HEADS, FOX_HEAD_DIM
    proj = x @ w_in
    q, k, v, f_pre = jnp.split(proj, [D_MODEL, 2 * D_MODEL, 3 * D_MODEL], axis=-1)
    q, k, v = _heads(q, H, dh), _heads(k, H, dh), _heads(v, H, dh)
    log_f = jax.nn.log_sigmoid((f_pre + b_f).astype(jnp.float32)).transpose(0, 2, 1)
    cum = jnp.cumsum(log_f, axis=-1)
    nb = S // Q_BLOCK
    qb = q.reshape(B, H, nb, Q_BLOCK, dh).transpose(2, 0, 1, 3, 4)
    cb = cum.reshape(B, H, nb, Q_BLOCK).transpose(2, 0, 1, 3)
    pos_k = jnp.arange(S)
    scale = dh ** -0.5

    def block(args):
        i, q_i, c_i = args
        pos_q = i * Q_BLOCK + jnp.arange(Q_BLOCK)
        logits = jnp.einsum('bhqd,bhkd->bhqk', q_i, k).astype(jnp.float32) * scale
        logits = logits + c_i[..., :, None] - cum[..., None, :]
        logits = jnp.where(pos_k[None, :] <= pos_q[:, None], logits, -jnp.inf)
        p = jax.nn.softmax(logits, axis=-1)
        return jnp.einsum('bhqk,bhkd->bhqd', p.astype(v.dtype), v)

    out = lax.map(block, (jnp.arange(nb), qb, cb))
    return (_blocks_to_seq(out) @ w_out).astype(x.dtype)


def rglru_block(x, w_in, conv_w, conv_b, w_r, b_r, w_i, b_i, lam, w_out):
    B, S, _ = x.shape
    W, NB, bd = LRU_WIDTH, LRU_BLOCKS, LRU_BLOCK_DIM
    proj = x @ w_in
    gate, u = jnp.split(proj, [W], axis=-1)
    u_pad = jnp.pad(u, ((0, 0), (CONV_WIDTH - 1, 0), (0, 0)))
    u = conv_b + sum(u_pad[:, j:j + S, :] * conv_w[j] for j in range(CONV_WIDTH))
    ub = u.reshape(B, S, NB, bd)
    r = jax.nn.sigmoid((jnp.einsum('bsnd,nde->bsne', ub, w_r).reshape(B, S, W) + b_r).astype(jnp.float32))
    ig = jax.nn.sigmoid((jnp.einsum('bsnd,nde->bsne', ub, w_i).reshape(B, S, W) + b_i).astype(jnp.float32))
    log_a = LRU_C * r * jax.nn.log_sigmoid(lam.astype(jnp.float32))
    a = jnp.exp(log_a)
    bterm = jnp.sqrt(-jnp.expm1(2.0 * log_a)) * (ig * u.astype(jnp.float32))

    def combine(left, right):
        a1, b1 = left
        a2, b2 = right
        return a1 * a2, a2 * b1 + b2

    _, h = lax.associative_scan(combine, (a, bterm), axis=1)
    y = h * jax.nn.gelu(gate.astype(jnp.float32))
    return (y.astype(x.dtype) @ w_out).astype(x.dtype)


def stick_breaking_attention(x, w_in, w_out):
    B, S, _ = x.shape
    H, dh = SB_HEADS, SB_HEAD_DIM
    q, k, v = jnp.split(x @ w_in, 3, axis=-1)
    q, k, v = _heads(q, H, dh), _heads(k, H, dh), _heads(v, H, dh)
    nb = S // Q_BLOCK
    qb = q.reshape(B, H, nb, Q_BLOCK, dh).transpose(2, 0, 1, 3, 4)
    pos_k = jnp.arange(S)
    scale = dh ** -0.5

    def block(args):
        i, q_i = args
        pos_q = i * Q_BLOCK + jnp.arange(Q_BLOCK)
        strict = pos_k[None, :] < pos_q[:, None]
        z = jnp.einsum('bhqd,bhkd->bhqk', q_i, k).astype(jnp.float32) * scale
        log_beta = jax.nn.log_sigmoid(z)
        log_1mb = jnp.where(strict, jax.nn.log_sigmoid(-z), 0.0)
        after = lax.cumsum(log_1mb, axis=3, reverse=True) - log_1mb
        att = jnp.where(strict, jnp.exp(log_beta + after), 0.0)
        return jnp.einsum('bhqk,bhkd->bhqd', att.astype(v.dtype), v)

    out = lax.map(block, (jnp.arange(nb), qb))
    return (_blocks_to_seq(out) @ w_out).astype(x.dtype)


def mlstm_block(x, w_in, b_if, head_g, w_out):
    B, S, _ = x.shape
    H, dk, dv, L = MLSTM_HEADS, MLSTM_QK_DIM, MLSTM_V_DIM, MLSTM_CHUNK
    qk = H * dk
    proj = x @ w_in
    q, k, v, o_pre, g_pre = jnp.split(proj, [qk, 2 * qk, 2 * qk + D_MODEL, 2 * qk + 2 * D_MODEL], axis=-1)
    q, k, v = _heads(q, H, dk), _heads(k, H, dk) * (dk ** -0.5), _heads(v, H, dv)
    g_pre = g_pre.astype(jnp.float32).reshape(B, S, 2, H) + b_if.astype(jnp.float32)
    i_pre = g_pre[:, :, 0].transpose(0, 2, 1)
    log_f = jax.nn.log_sigmoid(g_pre[:, :, 1]).transpose(0, 2, 1)
    nc = S // L

    def chunks(t):
        return jnp.moveaxis(t.reshape(t.shape[:2] + (nc, L) + t.shape[3:]), 2, 0)

    causal = jnp.tril(jnp.ones((L, L), dtype=bool))

    def step(carry, xs):
        C, n, m = carry
        qc, kc, vc, lfc, ic = xs
        b = jnp.cumsum(lfc, axis=-1)
        g = b[..., -1]
        Dm = jnp.where(causal, b[..., :, None] - b[..., None, :] + ic[..., None, :], -jnp.inf)
        inter = b + m[..., None]
        m_t = jnp.maximum(inter, jnp.max(Dm, axis=-1))
        w_intra = jnp.exp(Dm - m_t[..., None])
        w_inter = jnp.exp(inter - m_t)
        s = jnp.einsum('bhtd,bhsd->bhts', qc, kc).astype(jnp.float32) * w_intra
        num = (w_inter[..., None] * jnp.einsum('bhtd,bhde->bhte', qc, C)
               + jnp.einsum('bhts,bhse->bhte', s, vc))
        den = w_inter * jnp.einsum('bhtd,bhd->bht', qc, n) + jnp.sum(s, axis=-1)
        h = num / jnp.maximum(jnp.abs(den), jnp.exp(-m_t))[..., None]
        wk = g[..., None] - b + ic
        m_new = jnp.maximum(g + m, jnp.max(wk, axis=-1))
        decay = jnp.exp(g + m - m_new)
        wk_e = jnp.exp(wk - m_new[..., None])
        C_new = decay[..., None, None] * C + jnp.einsum('bhs,bhsd,bhse->bhde', wk_e, kc, vc)
        n_new = decay[..., None] * n + jnp.einsum('bhs,bhsd->bhd', wk_e, kc)
        return (C_new, n_new, m_new), h

    init = (jnp.zeros((B, H, dk, dv), jnp.float32), jnp.zeros((B, H, dk), jnp.float32),
            jnp.full((B, H), MLSTM_M_INIT, jnp.float32))
    _, hs = lax.scan(step, init, (chunks(q), chunks(k), chunks(v), chunks(log_f), chunks(i_pre)))
    h = hs.transpose(1, 0, 3, 2, 4).reshape(B, S, H, dv)
    hn = rmsnorm(h, head_g.reshape(H, dv)).reshape(B, S, D_MODEL)
    y = jax.nn.sigmoid(o_pre.astype(jnp.float32)) * hn
    return (y.astype(x.dtype) @ w_out).astype(x.dtype)


def sqrelu_mlp(x, w1, w2):
    return (jnp.square(jax.nn.relu(x @ w1)) @ w2).astype(x.dtype)


def _dense(key, shape, fan_in):
    return jax.random.normal(key, shape, jnp.float32) * (fan_in ** -0.5)


def setup_inputs(seed: int = 0) -> dict:
    key = jax.random.key(seed)
    ks = jax.random.split(key, 24)
    n0, n1, n2, n3 = _n_of(0), _n_of(1), _n_of(2), _n_of(3)
    D, W, NB, bd = D_MODEL, LRU_WIDTH, LRU_BLOCKS, LRU_BLOCK_DIM
    x = jax.random.normal(ks[0], (BATCH, SEQ, D), jnp.float32)
    norm_g = 1.0 + 0.02 * jax.random.normal(ks[1], (DEPTH, 4, D), jnp.float32)
    mlp_w1 = _dense(ks[2], (DEPTH, D, D_FF), D)
    mlp_w2 = _dense(ks[3], (DEPTH, D_FF, D), D_FF)
    fox_w_in = _dense(ks[4], (n0, D, 3 * D + FOX_HEADS), D)
    fox_b_f = 3.0 + 0.1 * jax.random.normal(ks[5], (n0, FOX_HEADS), jnp.float32)
    fox_w_out = _dense(ks[6], (n0, D, D), D)
    lru_w_in = _dense(ks[7], (n1, D, 2 * W), D)
    lru_conv_w = _dense(ks[8], (n1, CONV_WIDTH, W), CONV_WIDTH)
    lru_conv_b = 0.01 * jax.random.normal(ks[9], (n1, W), jnp.float32)
    lru_w_r = _dense(ks[10], (n1, NB, bd, bd), bd)
    lru_b_r = 0.01 * jax.random.normal(ks[11], (n1, W), jnp.float32)
    lru_w_i = _dense(ks[12], (n1, NB, bd, bd), bd)
    lru_b_i = 0.01 * jax.random.normal(ks[13], (n1, W), jnp.float32)
    a_c = jax.random.uniform(ks[14], (n1, W), jnp.float32, minval=0.9, maxval=0.999)
    p = a_c ** (1.0 / LRU_C)
    lru_lambda = jnp.log(p) - jnp.log1p(-p)
    lru_w_out = _dense(ks[15], (n1, W, D), W)
    sb_w_in = _dense(ks[16], (n2, D, 3 * D), D)
    sb_w_out = _dense(ks[17], (n2, D, D), D)
    mlstm_w_in = _dense(ks[18], (n3, D, 3 * D + 2 * MLSTM_HEADS), D)
    mlstm_b_if = jnp.stack([-1.0 + 0.1 * jax.random.normal(ks[19], (n3, MLSTM_HEADS), jnp.float32),
                            3.0 + 0.1 * jax.random.normal(ks[20], (n3, MLSTM_HEADS), jnp.float32)], axis=1)
    mlstm_head_g = 1.0 + 0.02 * jax.random.normal(ks[21], (n3, D), jnp.float32)
    mlstm_w_out = _dense(ks[22], (n3, D, D), D)
    return {'x': x, 'norm_g': norm_g, 'mlp_w1': mlp_w1, 'mlp_w2': mlp_w2,
            'fox_w_in': fox_w_in, 'fox_b_f': fox_b_f, 'fox_w_out': fox_w_out,
            'lru_w_in': lru_w_in, 'lru_conv_w': lru_conv_w, 'lru_conv_b': lru_conv_b,
            'lru_w_r': lru_w_r, 'lru_b_r': lru_b_r, 'lru_w_i': lru_w_i, 'lru_b_i': lru_b_i,
            'lru_lambda': lru_lambda, 'lru_w_out': lru_w_out,
            'sb_w_in': sb_w_in, 'sb_w_out': sb_w_out,
            'mlstm_w_in': mlstm_w_in, 'mlstm_b_if': mlstm_b_if, 'mlstm_head_g': mlstm_head_g,
            'mlstm_w_out': mlstm_w_out}


def reference(x, norm_g, mlp_w1, mlp_w2, fox_w_in, fox_b_f, fox_w_out,
              lru_w_in, lru_conv_w, lru_conv_b, lru_w_r, lru_b_r, lru_w_i, lru_b_i,
              lru_lambda, lru_w_out, sb_w_in, sb_w_out,
              mlstm_w_in, mlstm_b_if, mlstm_head_g, mlstm_w_out):
    for i in range(DEPTH):
        m, j = i % N_MIXERS, i // N_MIXERS
        h = rmsnorm(x, norm_g[i, 0])
        if m == 0:
            h = fox_attention(h, fox_w_in[j], fox_b_f[j], fox_w_out[j])
        elif m == 1:
            h = rglru_block(h, lru_w_in[j], lru_conv_w[j], lru_conv_b[j], lru_w_r[j], lru_b_r[j],
                            lru_w_i[j], lru_b_i[j], lru_lambda[j], lru_w_out[j])
        elif m == 2:
            h = stick_breaking_attention(h, sb_w_in[j], sb_w_out[j])
        else:
            h = mlstm_block(h, mlstm_w_in[j], mlstm_b_if[j], mlstm_head_g[j], mlstm_w_out[j])
        x = x + rmsnorm(h, norm_g[i, 1])
        h = sqrelu_mlp(rmsnorm(x, norm_g[i, 2]), mlp_w1[i], mlp_w2[i])
        x = x + rmsnorm(h, norm_g[i, 3])
    return x
```

```python
import functools
import math

import numpy as np
import jax
import jax.numpy as jnp
from jax import lax
from jax.experimental import pallas as pl
from jax.experimental.pallas import tpu as pltpu

F32 = jnp.float32
BF16 = jnp.bfloat16

EPS = 1e-6
LANES = 128
VMEM_LIMIT = 56 * 1024 * 1024
NEG = -1e30

FOX_HEADS = 16
SB_HEADS = 16
HEAD_DIM = 128
LRU_BLOCKS = 16
LRU_C = 8.0
MLSTM_HEADS = 4
MLSTM_M_INIT = -1e30


def _params(sem, vmem=VMEM_LIMIT):
    return pltpu.CompilerParams(dimension_semantics=sem, vmem_limit_bytes=vmem)


def _log_sigmoid(x):
    return jnp.minimum(x, 0.0) - jnp.log1p(jnp.exp(-jnp.abs(x)))


def _rms_scale(x):
    return lax.rsqrt(jnp.mean(x * x, axis=-1, keepdims=True) + EPS)


def _norm_proj_kernel(x_ref, g_ref, w_ref, o_ref, h_scr):
    @pl.when(pl.program_id(1) == 0)
    def _():
        x = x_ref[...]
        h_scr[...] = (x * _rms_scale(x) * g_ref[...]).astype(BF16)

    o_ref[...] = jnp.dot(h_scr[...], w_ref[...], preferred_element_type=F32).astype(o_ref.dtype)


def _norm_proj_gate_kernel(x_ref, g_ref, w_ref, wg_ref, o_ref, og_ref, h_scr):
    @pl.when(pl.program_id(1) == 0)
    def _():
        x = x_ref[...]
        h = (x * _rms_scale(x) * g_ref[...]).astype(BF16)
        h_scr[...] = h
        og_ref[...] = jnp.dot(h, wg_ref[...], preferred_element_type=F32)

    o_ref[...] = jnp.dot(h_scr[...], w_ref[...], preferred_element_type=F32).astype(o_ref.dtype)


def norm_proj(x, g, w, wg=None, *, out_dtype, tm, tn):
    m, d = x.shape
    n = w.shape[1]
    assert m % tm == 0 and n % tn == 0
    grid = (m // tm, n // tn)
    x_spec = pl.BlockSpec((tm, d), lambda i, j: (i, 0))
    g_spec = pl.BlockSpec((1, d), lambda i, j: (0, 0))
    w_spec = pl.BlockSpec((d, tn), lambda i, j: (0, j))
    o_spec = pl.BlockSpec((tm, tn), lambda i, j: (i, j))
    scratch = [pltpu.VMEM((tm, d), BF16)]
    if wg is None:
        return pl.pallas_call(
            _norm_proj_kernel, grid=grid,
            in_specs=[x_spec, g_spec, w_spec], out_specs=o_spec,
            out_shape=jax.ShapeDtypeStruct((m, n), out_dtype),
            scratch_shapes=scratch,
            compiler_params=_params(("parallel", "arbitrary")),
            name="norm_proj",
        )(x, g.reshape(1, d), w)
    ng = wg.shape[1]
    return pl.pallas_call(
        _norm_proj_gate_kernel, grid=grid,
        in_specs=[x_spec, g_spec, w_spec, pl.BlockSpec((d, ng), lambda i, j: (0, 0))],
        out_specs=[o_spec, pl.BlockSpec((tm, ng), lambda i, j: (i, 0))],
        out_shape=[jax.ShapeDtypeStruct((m, n), out_dtype), jax.ShapeDtypeStruct((m, ng), F32)],
        scratch_shapes=scratch,
        compiler_params=_params(("parallel", "arbitrary")),
        name="norm_proj_gate",
    )(x, g.reshape(1, d), w, wg)


def _proj_norm_res_kernel(a_ref, w_ref, g_ref, x_ref, o_ref):
    y = jnp.dot(a_ref[...], w_ref[...], preferred_element_type=F32)
    o_ref[...] = x_ref[...] + y * _rms_scale(y) * g_ref[...]


def proj_norm_res(a, w, g, x, *, tm):
    m, k = a.shape
    d = w.shape[1]
    assert m % tm == 0
    return pl.pallas_call(
        _proj_norm_res_kernel, grid=(m // tm,),
        in_specs=[pl.BlockSpec((tm, k), lambda i: (i, 0)),
                  pl.BlockSpec((k, d), lambda i: (0, 0)),
                  pl.BlockSpec((1, d), lambda i: (0, 0)),
                  pl.BlockSpec((tm, d), lambda i: (i, 0))],
        out_specs=pl.BlockSpec((tm, d), lambda i: (i, 0)),
        out_shape=jax.ShapeDtypeStruct((m, d), F32),
        compiler_params=_params(("parallel",)),
        name="proj_norm_res",
    )(a, w, g.reshape(1, d), x)


def _mlp_kernel(x_ref, g2_ref, w1_ref, w2_ref, g3_ref, o_ref, h_scr, acc_scr):
    f = pl.program_id(1)

    @pl.when(f == 0)
    def _():
        x = x_ref[...]
        h_scr[...] = (x * _rms_scale(x) * g2_ref[...]).astype(BF16)

    a = jnp.dot(h_scr[...], w1_ref[...], preferred_element_type=F32)
    a = jnp.square(jnp.maximum(a, 0.0)).astype(BF16)
    part = jnp.dot(a, w2_ref[...], preferred_element_type=F32)

    @pl.when(f == 0)
    def _():
        acc_scr[...] = part

    @pl.when(f > 0)
    def _():
        acc_scr[...] += part

    @pl.when(f == pl.num_programs(1) - 1)
    def _():
        y = acc_scr[...]
        o_ref[...] = x_ref[...] + y * _rms_scale(y) * g3_ref[...]


def mlp_block(x, g2, w1, w2, g3, *, tm, tf):
    m, d = x.shape
    ff = w1.shape[1]
    assert m % tm == 0 and ff % tf == 0
    return pl.pallas_call(
        _mlp_kernel, grid=(m // tm, ff // tf),
        in_specs=[pl.BlockSpec((tm, d), lambda i, f: (i, 0)),
                  pl.BlockSpec((1, d), lambda i, f: (0, 0)),
                  pl.BlockSpec((d, tf), lambda i, f: (0, f)),
                  pl.BlockSpec((tf, d), lambda i, f: (f, 0)),
                  pl.BlockSpec((1, d), lambda i, f: (0, 0))],
        out_specs=pl.BlockSpec((tm, d), lambda i, f: (i, 0)),
        out_shape=jax.ShapeDtypeStruct((m, d), F32),
        scratch_shapes=[pltpu.VMEM((tm, d), BF16), pltpu.VMEM((tm, d), F32)],
        compiler_params=_params(("parallel", "arbitrary")),
        name="mlp_block",
    )(x, g2.reshape(1, d), w1, w2, g3.reshape(1, d))


def _causal_pairs(nblk, descending):
    qi, kj = [], []
    for i in range(nblk):
        ks = range(i, -1, -1) if descending else range(i + 1)
        for j in ks:
            qi.append(i)
            kj.append(j)
    return jnp.asarray(np.array(qi, np.int32)), jnp.asarray(np.array(kj, np.int32))


def _fox_cum_kernel(f_ref, b_ref, c_ref, carry_scr):
    nblk = f_ref.shape[1] // LANES
    lane = lax.broadcasted_iota(jnp.int32, (f_ref.shape[0], LANES), 1)
    carry_scr[...] = jnp.zeros_like(carry_scr)

    def body(j, _):
        off = pl.multiple_of(j * LANES, LANES)
        v = _log_sigmoid(f_ref[:, pl.ds(off, LANES)] + b_ref[...])
        d = 1
        while d < LANES:
            v = v + jnp.where(lane >= d, pltpu.roll(v, d, 1), 0.0)
            d *= 2
        v = v + carry_scr[...]
        c_ref[:, pl.ds(off, LANES)] = v
        carry_scr[...] = v[:, LANES - 1:LANES]
        return 0

    lax.fori_loop(0, nblk, body, 0)


def fox_cum(f_t, b_f):
    b, h, s = f_t.shape
    return pl.pallas_call(
        _fox_cum_kernel, grid=(b,),
        in_specs=[pl.BlockSpec((None, h, s), lambda i: (i, 0, 0)),
                  pl.BlockSpec((h, 1), lambda i: (0, 0))],
        out_specs=pl.BlockSpec((None, h, s), lambda i: (i, 0, 0)),
        out_shape=jax.ShapeDtypeStruct((b, h, s), F32),
        scratch_shapes=[pltpu.VMEM((h, 1), F32)],
        compiler_params=_params(("parallel",)),
        name="fox_cum",
    )(f_t, b_f.reshape(h, 1))


def _fox_attn_kernel(qi_ref, kj_ref, q_ref, k_ref, v_ref, cq_ref, ck_ref, o_ref,
                     m_scr, l_scr, acc_scr, *, scale, blk):
    p = pl.program_id(2)
    qi = qi_ref[p]
    kj = kj_ref[p]

    @pl.when(kj == 0)
    def _():
        m_scr[...] = jnp.full_like(m_scr, NEG)
        l_scr[...] = jnp.zeros_like(l_scr)
        acc_scr[...] = jnp.zeros_like(acc_scr)

    s = lax.dot_general(q_ref[...], k_ref[...], (((1,), (1,)), ((), ())),
                        preferred_element_type=F32) * scale
    s = s + (cq_ref[:, 0:1] - ck_ref[...])
    row = lax.broadcasted_iota(jnp.int32, s.shape, 0)
    col = lax.broadcasted_iota(jnp.int32, s.shape, 1)
    s = jnp.where(col + (kj - qi) * blk <= row, s, NEG)

    m_prev = m_scr[...]
    m_new = jnp.maximum(m_prev, jnp.max(s, axis=-1, keepdims=True))
    alpha = jnp.exp(m_prev - m_new)
    pexp = jnp.exp(s - m_new)
    l_scr[...] = alpha * l_scr[...] + jnp.sum(pexp, axis=-1, keepdims=True)
    acc_scr[...] = alpha * acc_scr[...] + jnp.dot(pexp.astype(BF16), v_ref[...],
                                                  preferred_element_type=F32)
    m_scr[...] = m_new

    @pl.when(kj == qi)
    def _():
        o_ref[...] = (acc_scr[...] / l_scr[...]).astype(o_ref.dtype)


def fox_attention(proj, cum, *, batch, seq, blk):
    m = proj.shape[0]
    h, dh = FOX_HEADS, HEAD_DIM
    nblk = seq // blk
    qi_arr, kj_arr = _causal_pairs(nblk, descending=False)
    npairs = int(qi_arr.shape[0])
    kern = functools.partial(_fox_attn_kernel, scale=dh ** -0.5, blk=blk)
    grid_spec = pltpu.PrefetchScalarGridSpec(
        num_scalar_prefetch=2,
        grid=(batch, h, npairs),
        in_specs=[
            pl.BlockSpec((blk, dh), lambda b, hh, p, qi, kj: (b * nblk + qi[p], hh)),
            pl.BlockSpec((blk, dh), lambda b, hh, p, qi, kj: (b * nblk + kj[p], h + hh)),
            pl.BlockSpec((blk, dh), lambda b, hh, p, qi, kj: (b * nblk + kj[p], 2 * h + hh)),
            pl.BlockSpec((None, None, 1, blk), lambda b, hh, p, qi, kj: (b, hh, 0, qi[p])),
            pl.BlockSpec((None, None, 1, blk), lambda b, hh, p, qi, kj: (b, hh, 0, kj[p])),
        ],
        out_specs=pl.BlockSpec((blk, dh), lambda b, hh, p, qi, kj: (b * nblk + qi[p], hh)),
        scratch_shapes=[pltpu.VMEM((blk, 1), F32), pltpu.VMEM((blk, 1), F32),
                        pltpu.VMEM((blk, dh), F32)],
    )
    return pl.pallas_call(
        kern, grid_spec=grid_spec,
        out_shape=jax.ShapeDtypeStruct((m, h * dh), BF16),
        compiler_params=_params(("parallel", "parallel", "arbitrary")),
        name="fox_attention",
    )(qi_arr, kj_arr, proj, proj, proj, cum, cum)


def _sb_attn_kernel(qi_ref, kj_ref, q_ref, k_ref, v_ref, u_ref, o_ref,
                    carry_scr, acc_scr, *, scale, blk, sub):
    p = pl.program_id(2)
    qi = qi_ref[p]
    kj = kj_ref[p]

    @pl.when(kj == qi)
    def _():
        carry_scr[...] = jnp.zeros_like(carry_scr)
        acc_scr[...] = jnp.zeros_like(acc_scr)

    z = lax.dot_general(q_ref[...], k_ref[...], (((1,), (1,)), ((), ())),
                        preferred_element_type=F32) * scale
    row = lax.broadcasted_iota(jnp.int32, z.shape, 0)
    col = lax.broadcasted_iota(jnp.int32, z.shape, 1)
    strict = col + (kj - qi) * blk < row
    lp = jnp.log1p(jnp.exp(-jnp.abs(z)))
    log_beta = jnp.minimum(z, 0.0) - lp
    log_1mb = jnp.where(strict, log_beta - z, 0.0)

    carry = carry_scr[...]
    nsub = blk // sub
    att_parts = [None] * nsub
    for c in range(nsub - 1, -1, -1):
        sl = slice(c * sub, (c + 1) * sub)
        l_c = log_1mb[:, sl]
        hi = l_c.astype(BF16)
        lo = (l_c - hi.astype(F32)).astype(BF16)
        after = (jnp.dot(hi, u_ref[...], preferred_element_type=F32)
                 + jnp.dot(lo, u_ref[...], preferred_element_type=F32))
        a_c = jnp.exp(log_beta[:, sl] + after + carry)
        att_parts[c] = jnp.where(strict[:, sl], a_c, 0.0).astype(BF16)
        carry = carry + jnp.sum(l_c, axis=-1, keepdims=True)
    carry_scr[...] = carry
    att = jnp.concatenate(att_parts, axis=-1)
    acc_scr[...] += jnp.dot(att, v_ref[...], preferred_element_type=F32)

    @pl.when(kj == 0)
    def _():
        o_ref[...] = acc_scr[...].astype(o_ref.dtype)


def sb_attention(proj, *, batch, seq, blk, sub):
    m = proj.shape[0]
    h, dh = SB_HEADS, HEAD_DIM
    nblk = seq // blk
    qi_arr, kj_arr = _causal_pairs(nblk, descending=True)
    npairs = int(qi_arr.shape[0])
    u = jnp.asarray(np.tril(np.ones((sub, sub), np.float32), -1), BF16)
    kern = functools.partial(_sb_attn_kernel, scale=dh ** -0.5, blk=blk, sub=sub)
    grid_spec = pltpu.PrefetchScalarGridSpec(
        num_scalar_prefetch=2,
        grid=(batch, h, npairs),
        in_specs=[
            pl.BlockSpec((blk, dh), lambda b, hh, p, qi, kj: (b * nblk + qi[p], hh)),
            pl.BlockSpec((blk, dh), lambda b, hh, p, qi, kj: (b * nblk + kj[p], h + hh)),
            pl.BlockSpec((blk, dh), lambda b, hh, p, qi, kj: (b * nblk + kj[p], 2 * h + hh)),
            pl.BlockSpec((sub, sub), lambda b, hh, p, qi, kj: (0, 0)),
        ],
        out_specs=pl.BlockSpec((blk, dh), lambda b, hh, p, qi, kj: (b * nblk + qi[p], hh)),
        scratch_shapes=[pltpu.VMEM((blk, 1), F32), pltpu.VMEM((blk, dh), F32)],
    )
    return pl.pallas_call(
        kern, grid_spec=grid_spec,
        out_shape=jax.ShapeDtypeStruct((m, h * dh), BF16),
        compiler_params=_params(("parallel", "parallel", "arbitrary")),
        name="sb_attention",
    )(qi_arr, kj_arr, proj, proj, proj, u)


LRU_TN = 3 * LANES
LRU_KW = 7 * LANES
LRU_SEG = 8
LRU_HALO = 8


def _lru_windows(width, bd):
    starts = []
    for j in range(width // LRU_TN):
        c0, c1 = j * LRU_TN, (j + 1) * LRU_TN - 1
        r0, r1 = (c0 // bd) * bd, (c1 // bd + 1) * bd
        k0 = min((r0 // LANES) * LANES, width - LRU_KW)
        assert k0 <= r0 and r1 <= k0 + LRU_KW
        starts.append(k0)
    return starts


def _pack_lru_gates(w_r, w_i, width):
    nb, bd, _ = w_r.shape
    starts = _lru_windows(width, bd)
    eye = jnp.eye(nb, dtype=w_r.dtype)
    dense_r = jnp.einsum('nde,nm->ndme', w_r, eye).reshape(width, width)
    dense_i = jnp.einsum('nde,nm->ndme', w_i, eye).reshape(width, width)
    tiles = []
    for j, k0 in enumerate(starts):
        cs = slice(j * LRU_TN, (j + 1) * LRU_TN)
        tiles.append(jnp.concatenate([dense_r[k0:k0 + LRU_KW, cs], dense_i[k0:k0 + LRU_KW, cs]], axis=1))
    return jnp.stack(tiles).astype(BF16), starts


def _lru_kernel(gate_ref, u_ref, cw_ref, cb_ref, wri_ref, br_ref, bi_ref, lam_ref, y_ref,
                ubuf, a_scr, b_scr, h_scr, *, starts, t):
    sblk = pl.program_id(1)
    width = u_ref.shape[1]

    @pl.when(sblk == 0)
    def _():
        ubuf[0:LRU_HALO, :] = jnp.zeros((LRU_HALO, width), F32)
        h_scr[...] = jnp.zeros_like(h_scr)

    ubuf[LRU_HALO:LRU_HALO + t, :] = u_ref[...]
    nconv = cw_ref.shape[0]
    uc = cb_ref[...] + cw_ref[nconv - 1:nconv, :] * u_ref[...]
    for j in range(nconv - 1):
        off = LRU_HALO - (nconv - 1) + j
        uc = uc + cw_ref[j:j + 1, :] * ubuf[off:off + t, :]
    ubuf[0:LRU_HALO, :] = u_ref[t - LRU_HALO:t, :]

    ucb = uc.astype(BF16)
    log_sig_lam = _log_sigmoid(lam_ref[...])
    for j, k0 in enumerate(starts):
        cs = slice(j * LRU_TN, (j + 1) * LRU_TN)
        pre = jnp.dot(ucb[:, k0:k0 + LRU_KW], wri_ref[j], preferred_element_type=F32)
        r = jax.nn.sigmoid(pre[:, :LRU_TN] + br_ref[:, cs])
        ig = jax.nn.sigmoid(pre[:, LRU_TN:] + bi_ref[:, cs])
        log_a = LRU_C * r * log_sig_lam[:, cs]
        a = jnp.exp(log_a)
        bterm = jnp.sqrt(1.0 - a * a) * (ig * uc[:, cs])
        for c in range(LRU_TN // LANES):
            ls = slice(c * LANES, (c + 1) * LANES)
            a_scr[j * (LRU_TN // LANES) + c] = a[:, ls]
            b_scr[j * (LRU_TN // LANES) + c] = bterm[:, ls]

    seg_len = t // LRU_SEG
    seg_id = lax.broadcasted_iota(jnp.int32, (LRU_SEG, LANES), 0)
    for c in range(width // LANES):
        ls = slice(c * LANES, (c + 1) * LANES)
        hloc = jnp.zeros((LRU_SEG, LANES), F32)
        prod = jnp.ones((LRU_SEG, LANES), F32)
        for i in range(seg_len):
            rows = pl.ds(i, LRU_SEG, stride=seg_len)
            a_i = a_scr[c, rows, :]
            hloc = a_i * hloc + b_scr[c, rows, :]
            prod = a_i * prod
            b_scr[c, rows, :] = hloc
            a_scr[c, rows, :] = prod
        carry = h_scr[:, ls]
        cin = jnp.zeros((LRU_SEG, LANES), F32)
        for sgm in range(LRU_SEG):
            cin = jnp.where(seg_id == sgm, carry, cin)
            carry = hloc[sgm:sgm + 1, :] + prod[sgm:sgm + 1, :] * carry
        h_scr[:, ls] = carry
        for i in range(seg_len):
            rows = pl.ds(i, LRU_SEG, stride=seg_len)
            b_scr[c, rows, :] = b_scr[c, rows, :] + a_scr[c, rows, :] * cin
        y_ref[:, ls] = (b_scr[c] * jax.nn.gelu(gate_ref[:, ls], approximate=True)).astype(y_ref.dtype)


def lru_core(proj, conv_w, conv_b, wri, starts, b_r, b_i, lam, *, batch, seq, t):
    m = proj.shape[0]
    width = proj.shape[1] // 2
    nt = seq // t
    assert seq % t == 0 and t % LRU_SEG == 0
    vec = lambda: pl.BlockSpec((1, width), lambda b, s: (0, 0))
    kern = functools.partial(_lru_kernel, starts=tuple(starts), t=t)
    return pl.pallas_call(
        kern, grid=(batch, nt),
        in_specs=[pl.BlockSpec((t, width), lambda b, s: (b * nt + s, 0)),
                  pl.BlockSpec((t, width), lambda b, s: (b * nt + s, 1)),
                  pl.BlockSpec(conv_w.shape, lambda b, s: (0, 0)),
                  vec(),
                  pl.BlockSpec(wri.shape, lambda b, s: (0, 0, 0)),
                  vec(), vec(), vec()],
        out_specs=pl.BlockSpec((t, width), lambda b, s: (b * nt + s, 0)),
        out_shape=jax.ShapeDtypeStruct((m, width), BF16),
        scratch_shapes=[pltpu.VMEM((t + LRU_HALO, width), F32),
                        pltpu.VMEM((width // LANES, t, LANES), F32),
                        pltpu.VMEM((width // LANES, t, LANES), F32),
                        pltpu.VMEM((1, width), F32)],
        compiler_params=_params(("parallel", "arbitrary")),
        name="lru_core",
    )(proj, proj, conv_w, conv_b.reshape(1, width), wri, b_r.reshape(1, width),
      b_i.reshape(1, width), lam.reshape(1, width))


def _col_to_row(col, eye):
    return jnp.sum(jnp.where(eye, col, 0.0), axis=0, keepdims=True)


def _mlstm_kernel(q_ref, k_ref, v_ref, o_ref, g_ref, bif_ref, hg_ref, y_ref,
                  c_scr, n_scr, m_scr, *, nheads, kscale):
    head = pl.program_id(1)
    chunk = pl.program_id(2)
    length = q_ref.shape[0]

    @pl.when(chunk == 0)
    def _():
        c_scr[...] = jnp.zeros_like(c_scr)
        n_scr[...] = jnp.zeros_like(n_scr)
        m_scr[...] = jnp.full_like(m_scr, MLSTM_M_INIT)

    row = lax.broadcasted_iota(jnp.int32, (length, length), 0)
    col = lax.broadcasted_iota(jnp.int32, (length, length), 1)
    eye = row == col
    causal = col <= row

    g = g_ref[...] + bif_ref[...]
    lane = lax.broadcasted_iota(jnp.int32, g.shape, 1)
    i_col = jnp.sum(jnp.where(lane == head, g, 0.0), axis=-1, keepdims=True)
    f_col = jnp.sum(jnp.where(lane == nheads + head, g, 0.0), axis=-1, keepdims=True)
    lf_col = _log_sigmoid(f_col)
    lf_row = _col_to_row(lf_col, eye)
    i_row = _col_to_row(i_col, eye)
    b_col = jnp.sum(jnp.where(causal, lf_row, 0.0), axis=-1, keepdims=True)
    b_row = _col_to_row(b_col, eye)
    gsum = jnp.sum(lf_row, axis=-1, keepdims=True)

    m_prev = m_scr[...]
    dm = jnp.where(causal, b_col - b_row + i_row, NEG)
    inter = b_col + m_prev
    m_t = jnp.maximum(inter, jnp.max(dm, axis=-1, keepdims=True))
    w_intra = jnp.exp(dm - m_t)
    w_inter = jnp.exp(inter - m_t)

    q = q_ref[...]
    ks = k_ref[...] * kscale
    v = v_ref[...]
    s = lax.dot_general(q, ks, (((1,), (1,)), ((), ())), preferred_element_type=F32) * w_intra
    num = (w_inter * jnp.dot(q, c_scr[...].astype(BF16), preferred_element_type=F32)
           + jnp.dot(s.astype(BF16), v, preferred_element_type=F32))
    qn = jnp.sum(q.astype(F32) * n_scr[...], axis=-1, keepdims=True)
    den = w_inter * qn + jnp.sum(s, axis=-1, keepdims=True)
    hval = num / jnp.maximum(jnp.abs(den), jnp.exp(-m_t))

    hn = hval * _rms_scale(hval) * hg_ref[...]
    y_ref[...] = (jax.nn.sigmoid(o_ref[...].astype(F32)) * hn).astype(y_ref.dtype)

    wk_col = gsum - b_col + i_col
    m_new = jnp.maximum(gsum + m_prev, jnp.max(wk_col, axis=0, keepdims=True))
    decay = jnp.exp(gsum + m_prev - m_new)
    kw = jnp.exp(wk_col - m_new) * ks.astype(F32)
    c_scr[...] = decay * c_scr[...] + lax.dot_general(
        kw.astype(BF16), v, (((0,), (0,)), ((), ())), preferred_element_type=F32)
    n_scr[...] = decay * n_scr[...] + jnp.sum(kw, axis=0, keepdims=True)
    m_scr[...] = m_new


def mlstm_core(proj, gates, b_if, head_g, *, batch, seq, chunk):
    m = proj.shape[0]
    d = head_g.shape[0]
    nh = MLSTM_HEADS
    dk, dv = d // 2 // nh, d // nh
    nc = seq // chunk
    ng = gates.shape[1]
    bif = jnp.zeros((1, ng), F32).at[0, :2 * nh].set(b_if.reshape(-1))
    kern = functools.partial(_mlstm_kernel, nheads=nh, kscale=dk ** -0.5)
    kblk0, vblk0, oblk0 = (nh * dk) // dk, (2 * nh * dk) // dv, (2 * nh * dk + d) // dv
    return pl.pallas_call(
        kern, grid=(batch, nh, nc),
        in_specs=[pl.BlockSpec((chunk, dk), lambda b, h, c: (b * nc + c, h)),
                  pl.BlockSpec((chunk, dk), lambda b, h, c: (b * nc + c, kblk0 + h)),
                  pl.BlockSpec((chunk, dv), lambda b, h, c: (b * nc + c, vblk0 + h)),
                  pl.BlockSpec((chunk, dv), lambda b, h, c: (b * nc + c, oblk0 + h)),
                  pl.BlockSpec((chunk, ng), lambda b, h, c: (b * nc + c, 0)),
                  pl.BlockSpec((1, ng), lambda b, h, c: (0, 0)),
                  pl.BlockSpec((1, dv), lambda b, h, c: (0, h))],
        out_specs=pl.BlockSpec((chunk, dv), lambda b, h, c: (b * nc + c, h)),
        out_shape=jax.ShapeDtypeStruct((m, d), BF16),
        scratch_shapes=[pltpu.VMEM((dk, dv), F32), pltpu.VMEM((1, dk), F32), pltpu.VMEM((1, 1), F32)],
        compiler_params=_params(("parallel", "parallel", "arbitrary")),
        name="mlstm_core",
    )(proj, proj, proj, proj, gates, bif, head_g.reshape(1, d))


def _pad_cols(w, n):
    return jnp.pad(w, ((0, 0), (0, n - w.shape[1])))


def kernel(x, norm_g, mlp_w1, mlp_w2, fox_w_in, fox_b_f, fox_w_out, lru_w_in, lru_conv_w, lru_conv_b,
           lru_w_r, lru_b_r, lru_w_i, lru_b_i, lru_lambda, lru_w_out, sb_w_in, sb_w_out,
           mlstm_w_in, mlstm_b_if, mlstm_head_g, mlstm_w_out):
    batch, seq, d = x.shape
    depth = norm_g.shape[0]
    n_mixers = 4
    xf = x.reshape(batch * seq, d)
    tm_proj = 1024
    tm_out = 512
    attn_blk = min(512, seq)

    for i in range(depth):
        mixer, j = i % n_mixers, i // n_mixers
        if mixer == 0:
            w = fox_w_in[j]
            proj, fgate = norm_proj(xf, norm_g[i, 0], w[:, :3 * d].astype(BF16),
                                    _pad_cols(w[:, 3 * d:], LANES).astype(BF16),
                                    out_dtype=BF16, tm=tm_proj, tn=1024)
            f_t = fgate[:, :FOX_HEADS].reshape(batch, seq, FOX_HEADS).transpose(0, 2, 1)
            cum = fox_cum(f_t, fox_b_f[j]).reshape(batch, FOX_HEADS, 1, seq)
            mixed = fox_attention(proj, cum, batch=batch, seq=seq, blk=attn_blk)
            w_out = fox_w_out[j]
        elif mixer == 1:
            width = lru_w_in.shape[2] // 2
            proj = norm_proj(xf, norm_g[i, 0], lru_w_in[j].astype(BF16), out_dtype=F32,
                             tm=tm_proj, tn=width // 3)
            wri, starts = _pack_lru_gates(lru_w_r[j], lru_w_i[j], width)
            mixed = lru_core(proj, lru_conv_w[j], lru_conv_b[j], wri, starts, lru_b_r[j], lru_b_i[j],
                             lru_lambda[j], batch=batch, seq=seq, t=min(256, seq))
            w_out = lru_w_out[j]
        elif mixer == 2:
            proj = norm_proj(xf, norm_g[i, 0], sb_w_in[j].astype(BF16), out_dtype=BF16,
                             tm=tm_proj, tn=1024)
            mixed = sb_attention(proj, batch=batch, seq=seq, blk=attn_blk, sub=256)
            w_out = sb_w_out[j]
        else:
            w = mlstm_w_in[j]
            proj, gates = norm_proj(xf, norm_g[i, 0], w[:, :3 * d].astype(BF16),
                                    _pad_cols(w[:, 3 * d:], LANES).astype(BF16),
                                    out_dtype=BF16, tm=tm_proj, tn=1024)
            mixed = mlstm_core(proj, gates, mlstm_b_if[j], mlstm_head_g[j],
                               batch=batch, seq=seq, chunk=min(256, seq))
            w_out = mlstm_w_out[j]
        xf = proj_norm_res(mixed, w_out.astype(BF16), norm_g[i, 1], xf, tm=tm_out)
        xf = mlp_block(xf, norm_g[i, 2], mlp_w1[i].astype(BF16), mlp_w2[i].astype(BF16),
                       norm_g[i, 3], tm=512, tf=1024)
    return xf.reshape(batch, seq, d)
```

```python
import functools
import math

import numpy as np
import jax
import jax.numpy as jnp
from jax import lax
from jax.experimental import pallas as pl
from jax.experimental.pallas import tpu as pltpu

F32 = jnp.float32
BF16 = jnp.bfloat16

EPS = 1e-6
LANES = 128
VMEM_LIMIT = 56 * 1024 * 1024
NEG = -1e30
LOG2E = math.log2(math.e)

FOX_HEADS = 16
SB_HEADS = 16
HEAD_DIM = 128
LRU_BLOCKS = 16
LRU_C = 8.0
MLSTM_HEADS = 4
MLSTM_M_INIT = -1e30


def _params(sem, vmem=VMEM_LIMIT):
    return pltpu.CompilerParams(dimension_semantics=sem, vmem_limit_bytes=vmem)


def _log_sigmoid(x):
    return jnp.minimum(x, 0.0) - jnp.log1p(jnp.exp(-jnp.abs(x)))


def _rms_scale(x):
    return lax.rsqrt(jnp.mean(x * x, axis=-1, keepdims=True) + EPS)


def _project(h_scr, w_ref, o_ref, q_scale, q_tiles):
    acc = jnp.dot(h_scr[...], w_ref[...], preferred_element_type=F32)
    if q_tiles:
        acc = acc * jnp.where(pl.program_id(1) < q_tiles, q_scale, 1.0)
    o_ref[...] = acc.astype(o_ref.dtype)


def _norm_proj_kernel(x_ref, g_ref, w_ref, o_ref, h_scr, *, q_scale, q_tiles):
    @pl.when(pl.program_id(1) == 0)
    def _():
        x = x_ref[...]
        h_scr[...] = (x * _rms_scale(x) * g_ref[...]).astype(BF16)

    _project(h_scr, w_ref, o_ref, q_scale, q_tiles)


def _norm_proj_gate_kernel(x_ref, g_ref, w_ref, wg_ref, o_ref, og_ref, h_scr, *, q_scale, q_tiles):
    @pl.when(pl.program_id(1) == 0)
    def _():
        x = x_ref[...]
        h = (x * _rms_scale(x) * g_ref[...]).astype(BF16)
        h_scr[...] = h
        og_ref[...] = jnp.dot(h, wg_ref[...], preferred_element_type=F32)

    _project(h_scr, w_ref, o_ref, q_scale, q_tiles)


def norm_proj(x, g, w, wg=None, *, out_dtype, tm, tn, q_scale=1.0, q_cols=0):
    m, d = x.shape
    n = w.shape[1]
    assert m % tm == 0 and n % tn == 0 and q_cols % tn == 0
    grid = (m // tm, n // tn)
    x_spec = pl.BlockSpec((tm, d), lambda i, j: (i, 0))
    g_spec = pl.BlockSpec((1, d), lambda i, j: (0, 0))
    w_spec = pl.BlockSpec((d, tn), lambda i, j: (0, j))
    o_spec = pl.BlockSpec((tm, tn), lambda i, j: (i, j))
    scratch = [pltpu.VMEM((tm, d), BF16)]
    statics = dict(q_scale=q_scale, q_tiles=q_cols // tn)
    if wg is None:
        return pl.pallas_call(
            functools.partial(_norm_proj_kernel, **statics), grid=grid,
            in_specs=[x_spec, g_spec, w_spec], out_specs=o_spec,
            out_shape=jax.ShapeDtypeStruct((m, n), out_dtype),
            scratch_shapes=scratch,
            compiler_params=_params(("parallel", "arbitrary")),
            name="norm_proj",
        )(x, g.reshape(1, d), w)
    ng = wg.shape[1]
    return pl.pallas_call(
        functools.partial(_norm_proj_gate_kernel, **statics), grid=grid,
        in_specs=[x_spec, g_spec, w_spec, pl.BlockSpec((d, ng), lambda i, j: (0, 0))],
        out_specs=[o_spec, pl.BlockSpec((tm, ng), lambda i, j: (i, 0))],
        out_shape=[jax.ShapeDtypeStruct((m, n), out_dtype), jax.ShapeDtypeStruct((m, ng), F32)],
        scratch_shapes=scratch,
        compiler_params=_params(("parallel", "arbitrary")),
        name="norm_proj_gate",
    )(x, g.reshape(1, d), w, wg)


def _proj_norm_res_kernel(a_ref, w_ref, g_ref, x_ref, o_ref):
    y = jnp.dot(a_ref[...], w_ref[...], preferred_element_type=F32)
    o_ref[...] = x_ref[...] + y * _rms_scale(y) * g_ref[...]


def proj_norm_res(a, w, g, x, *, tm):
    m, k = a.shape
    d = w.shape[1]
    assert m % tm == 0
    return pl.pallas_call(
        _proj_norm_res_kernel, grid=(m // tm,),
        in_specs=[pl.BlockSpec((tm, k), lambda i: (i, 0)),
                  pl.BlockSpec((k, d), lambda i: (0, 0)),
                  pl.BlockSpec((1, d), lambda i: (0, 0)),
                  pl.BlockSpec((tm, d), lambda i: (i, 0))],
        out_specs=pl.BlockSpec((tm, d), lambda i: (i, 0)),
        out_shape=jax.ShapeDtypeStruct((m, d), F32),
        compiler_params=_params(("parallel",)),
        name="proj_norm_res",
    )(a, w, g.reshape(1, d), x)


def _mlp_kernel(x_ref, g2_ref, w1_ref, w2_ref, g3_ref, o_ref, h_scr, acc_scr):
    f = pl.program_id(1)

    @pl.when(f == 0)
    def _():
        x = x_ref[...]
        h_scr[...] = (x * _rms_scale(x) * g2_ref[...]).astype(BF16)

    a = jnp.dot(h_scr[...], w1_ref[...], preferred_element_type=F32)
    a = jnp.square(jnp.maximum(a, 0.0)).astype(BF16)
    part = jnp.dot(a, w2_ref[...], preferred_element_type=F32)

    @pl.when(f == 0)
    def _():
        acc_scr[...] = part

    @pl.when(f > 0)
    def _():
        acc_scr[...] += part

    @pl.when(f == pl.num_programs(1) - 1)
    def _():
        y = acc_scr[...]
        o_ref[...] = x_ref[...] + y * _rms_scale(y) * g3_ref[...]


def mlp_block(x, g2, w1, w2, g3, *, tm, tf):
    m, d = x.shape
    ff = w1.shape[1]
    assert m % tm == 0 and ff % tf == 0
    return pl.pallas_call(
        _mlp_kernel, grid=(m // tm, ff // tf),
        in_specs=[pl.BlockSpec((tm, d), lambda i, f: (i, 0)),
                  pl.BlockSpec((1, d), lambda i, f: (0, 0)),
                  pl.BlockSpec((d, tf), lambda i, f: (0, f)),
                  pl.BlockSpec((tf, d), lambda i, f: (f, 0)),
                  pl.BlockSpec((1, d), lambda i, f: (0, 0))],
        out_specs=pl.BlockSpec((tm, d), lambda i, f: (i, 0)),
        out_shape=jax.ShapeDtypeStruct((m, d), F32),
        scratch_shapes=[pltpu.VMEM((tm, d), BF16), pltpu.VMEM((tm, d), F32)],
        compiler_params=_params(("parallel", "arbitrary")),
        name="mlp_block",
    )(x, g2.reshape(1, d), w1, w2, g3.reshape(1, d))


def _fox_cum_kernel(f_ref, b_ref, c_ref, carry_scr):
    nblk = f_ref.shape[1] // LANES
    lane = lax.broadcasted_iota(jnp.int32, (f_ref.shape[0], LANES), 1)
    carry_scr[...] = jnp.zeros_like(carry_scr)

    def body(j, _):
        off = pl.multiple_of(j * LANES, LANES)
        v = _log_sigmoid(f_ref[:, pl.ds(off, LANES)] + b_ref[...])
        d = 1
        while d < LANES:
            v = v + jnp.where(lane >= d, pltpu.roll(v, d, 1), 0.0)
            d *= 2
        v = v + carry_scr[...]
        c_ref[:, pl.ds(off, LANES)] = v * LOG2E
        carry_scr[...] = v[:, LANES - 1:LANES]
        return 0

    lax.fori_loop(0, nblk, body, 0)


def fox_cum(f_t, b_f):
    b, h, s = f_t.shape
    return pl.pallas_call(
        _fox_cum_kernel, grid=(b,),
        in_specs=[pl.BlockSpec((None, h, s), lambda i: (i, 0, 0)),
                  pl.BlockSpec((h, 1), lambda i: (0, 0))],
        out_specs=pl.BlockSpec((None, h, s), lambda i: (i, 0, 0)),
        out_shape=jax.ShapeDtypeStruct((b, h, s), F32),
        scratch_shapes=[pltpu.VMEM((h, 1), F32)],
        compiler_params=_params(("parallel",)),
        name="fox_cum",
    )(f_t, b_f.reshape(h, 1))


def _fox_attn_kernel(q_ref, k_ref, v_ref, c_ref, o_ref, m_scr, l_scr, acc_scr, *, blk):
    qi = pl.program_id(2)
    q = q_ref[...]
    q0 = pl.multiple_of(qi * blk, blk)
    c0 = c_ref[:, pl.ds(q0, LANES)][:, 0:1]
    m_scr[...] = jnp.full_like(m_scr, NEG)
    l_scr[...] = jnp.zeros_like(l_scr)
    acc_scr[...] = jnp.zeros_like(acc_scr)

    def step(kstart, diagonal):
        k = k_ref[pl.ds(kstart, blk), :]
        v = v_ref[pl.ds(kstart, blk), :]
        s = lax.dot_general(q, k, (((1,), (1,)), ((), ())), preferred_element_type=F32)
        s = s + (c0 - c_ref[:, pl.ds(kstart, blk)])
        if diagonal:
            row = lax.broadcasted_iota(jnp.int32, s.shape, 0)
            col = lax.broadcasted_iota(jnp.int32, s.shape, 1)
            s = jnp.where(col <= row, s, NEG)
        m_prev = m_scr[...]
        m_new = jnp.maximum(m_prev, jnp.max(s, axis=-1, keepdims=True))
        p = jnp.exp2(s - pltpu.repeat(m_new, blk // LANES, axis=1))
        alpha = jnp.exp2(m_prev - m_new)
        l_scr[...] = alpha * l_scr[...] + jnp.sum(p, axis=-1, keepdims=True)
        acc_scr[...] = alpha * acc_scr[...] + jnp.dot(p.astype(BF16), v, preferred_element_type=F32)
        m_scr[...] = m_new

    def body(j, carry):
        step(pl.multiple_of(j * blk, blk), False)
        return carry

    lax.fori_loop(0, qi, body, 0)
    step(q0, True)
    o_ref[...] = (acc_scr[...] / l_scr[...]).astype(o_ref.dtype)


def fox_attention(proj, cum, *, batch, seq, blk):
    m = proj.shape[0]
    h, dh = FOX_HEADS, HEAD_DIM
    assert dh == LANES and seq % blk == 0
    nblk = seq // blk
    kern = functools.partial(_fox_attn_kernel, blk=blk)
    return pl.pallas_call(
        kern, grid=(batch, h, nblk),
        in_specs=[
            pl.BlockSpec((blk, dh), lambda b, hh, i: (b * nblk + i, hh)),
            pl.BlockSpec((seq, dh), lambda b, hh, i: (b, h + hh)),
            pl.BlockSpec((seq, dh), lambda b, hh, i: (b, 2 * h + hh)),
            pl.BlockSpec((None, None, 1, seq), lambda b, hh, i: (b, hh, 0, 0)),
        ],
        out_specs=pl.BlockSpec((blk, dh), lambda b, hh, i: (b * nblk + i, hh)),
        out_shape=jax.ShapeDtypeStruct((m, h * dh), BF16),
        scratch_shapes=[pltpu.VMEM((blk, LANES), F32), pltpu.VMEM((blk, LANES), F32),
                        pltpu.VMEM((blk, dh), F32)],
        compiler_params=_params(("parallel", "parallel", "arbitrary")),
        name="fox_attention",
    )(proj, proj, proj, cum)


def _sb_attn_kernel(q_ref, k_ref, v_ref, u_ref, o_ref, carry_scr, acc_scr, *, blk, sub):
    qi = pl.program_id(2)
    q = q_ref[...]
    q0 = pl.multiple_of(qi * blk, blk)
    carry_scr[...] = jnp.zeros_like(carry_scr)
    acc_scr[...] = jnp.zeros_like(acc_scr)
    nsub = blk // sub

    def step(kstart, diagonal):
        k = k_ref[pl.ds(kstart, blk), :]
        v = v_ref[pl.ds(kstart, blk), :]
        z = lax.dot_general(q, k, (((1,), (1,)), ((), ())), preferred_element_type=F32)
        sp = jnp.maximum(z, 0.0) + jnp.log(1.0 + jnp.exp2(-jnp.abs(z))) * LOG2E
        if diagonal:
            row = lax.broadcasted_iota(jnp.int32, z.shape, 0)
            col = lax.broadcasted_iota(jnp.int32, z.shape, 1)
            strict = col < row
            sp = jnp.where(strict, sp, 0.0)
        carry = carry_scr[...]
        parts = [None] * nsub
        for c in range(nsub - 1, -1, -1):
            sl = slice(c * sub, (c + 1) * sub)
            sp_c = sp[:, sl]
            hi = sp_c.astype(BF16)
            lo = (sp_c - hi.astype(F32)).astype(BF16)
            g = jnp.dot(jnp.concatenate([hi, lo], axis=1), u_ref[...], preferred_element_type=F32)
            a_c = jnp.exp2(z[:, sl] - g - pltpu.repeat(carry, sub // LANES, axis=1))
            if diagonal:
                a_c = jnp.where(strict[:, sl], a_c, 0.0)
            parts[c] = a_c.astype(BF16)
            carry = carry + jnp.sum(sp_c, axis=-1, keepdims=True)
        carry_scr[...] = carry
        acc_scr[...] += jnp.dot(jnp.concatenate(parts, axis=1), v, preferred_element_type=F32)

    step(q0, True)

    def body(j, c):
        step(pl.multiple_of((qi - 1 - j) * blk, blk), False)
        return c

    lax.fori_loop(0, qi, body, 0)
    o_ref[...] = acc_scr[...].astype(o_ref.dtype)


def sb_attention(proj, *, batch, seq, blk, sub):
    m = proj.shape[0]
    h, dh = SB_HEADS, HEAD_DIM
    assert seq % blk == 0 and blk % sub == 0
    nblk = seq // blk
    tri = np.tril(np.ones((sub, sub), np.float32))
    u = jnp.asarray(np.concatenate([tri, tri], axis=0), BF16)
    kern = functools.partial(_sb_attn_kernel, blk=blk, sub=sub)
    return pl.pallas_call(
        kern, grid=(batch, h, nblk),
        in_specs=[
            pl.BlockSpec((blk, dh), lambda b, hh, i: (b * nblk + i, hh)),
            pl.BlockSpec((seq, dh), lambda b, hh, i: (b, h + hh)),
            pl.BlockSpec((seq, dh), lambda b, hh, i: (b, 2 * h + hh)),
            pl.BlockSpec((2 * sub, sub), lambda b, hh, i: (0, 0)),
        ],
        out_specs=pl.BlockSpec((blk, dh), lambda b, hh, i: (b * nblk + i, hh)),
        out_shape=jax.ShapeDtypeStruct((m, h * dh), BF16),
        scratch_shapes=[pltpu.VMEM((blk, LANES), F32), pltpu.VMEM((blk, dh), F32)],
        compiler_params=_params(("parallel", "parallel", "arbitrary")),
        name="sb_attention",
    )(proj, proj, proj, u)


LRU_TN = 3 * LANES
LRU_KW = 7 * LANES
LRU_SEG = 8
LRU_HALO = 8


def _lru_windows(width, bd):
    starts = []
    for j in range(width // LRU_TN):
        c0, c1 = j * LRU_TN, (j + 1) * LRU_TN - 1
        r0, r1 = (c0 // bd) * bd, (c1 // bd + 1) * bd
        k0 = min((r0 // LANES) * LANES, width - LRU_KW)
        assert k0 <= r0 and r1 <= k0 + LRU_KW
        starts.append(k0)
    return starts


def _pack_lru_gates(w_r, w_i, width):
    nb, bd, _ = w_r.shape
    starts = _lru_windows(width, bd)
    eye = jnp.eye(nb, dtype=w_r.dtype)
    dense_r = jnp.einsum('nde,nm->ndme', w_r, eye).reshape(width, width)
    dense_i = jnp.einsum('nde,nm->ndme', w_i, eye).reshape(width, width)
    tiles = []
    for j, k0 in enumerate(starts):
        cs = slice(j * LRU_TN, (j + 1) * LRU_TN)
        tiles.append(jnp.concatenate([dense_r[k0:k0 + LRU_KW, cs], dense_i[k0:k0 + LRU_KW, cs]], axis=1))
    return jnp.stack(tiles).astype(BF16), starts


def _lru_kernel(gate_ref, u_ref, cw_ref, cb_ref, wri_ref, br_ref, bi_ref, lam_ref, y_ref,
                ubuf, a_scr, b_scr, h_scr, *, starts, t):
    sblk = pl.program_id(1)
    width = u_ref.shape[1]

    @pl.when(sblk == 0)
    def _():
        ubuf[0:LRU_HALO, :] = jnp.zeros((LRU_HALO, width), F32)
        h_scr[...] = jnp.zeros_like(h_scr)

    ubuf[LRU_HALO:LRU_HALO + t, :] = u_ref[...]
    nconv = cw_ref.shape[0]
    uc = cb_ref[...] + cw_ref[nconv - 1:nconv, :] * u_ref[...]
    for j in range(nconv - 1):
        off = LRU_HALO - (nconv - 1) + j
        uc = uc + cw_ref[j:j + 1, :] * ubuf[off:off + t, :]
    ubuf[0:LRU_HALO, :] = u_ref[t - LRU_HALO:t, :]

    ucb = uc.astype(BF16)
    log_sig_lam = _log_sigmoid(lam_ref[...])
    for j, k0 in enumerate(starts):
        cs = slice(j * LRU_TN, (j + 1) * LRU_TN)
        pre = jnp.dot(ucb[:, k0:k0 + LRU_KW], wri_ref[j], preferred_element_type=F32)
        r = jax.nn.sigmoid(pre[:, :LRU_TN] + br_ref[:, cs])
        ig = jax.nn.sigmoid(pre[:, LRU_TN:] + bi_ref[:, cs])
        log_a = LRU_C * r * log_sig_lam[:, cs]
        a = jnp.exp(log_a)
        bterm = jnp.sqrt(1.0 - a * a) * (ig * uc[:, cs])
        for c in range(LRU_TN // LANES):
            ls = slice(c * LANES, (c + 1) * LANES)
            a_scr[j * (LRU_TN // LANES) + c] = a[:, ls]
            b_scr[j * (LRU_TN // LANES) + c] = bterm[:, ls]

    seg_len = t // LRU_SEG
    seg_id = lax.broadcasted_iota(jnp.int32, (LRU_SEG, LANES), 0)
    for c in range(width // LANES):
        ls = slice(c * LANES, (c + 1) * LANES)
        hloc = jnp.zeros((LRU_SEG, LANES), F32)
        prod = jnp.ones((LRU_SEG, LANES), F32)
        for i in range(seg_len):
            rows = pl.ds(i, LRU_SEG, stride=seg_len)
            a_i = a_scr[c, rows, :]
            hloc = a_i * hloc + b_scr[c, rows, :]
            prod = a_i * prod
            b_scr[c, rows, :] = hloc
            a_scr[c, rows, :] = prod
        carry = h_scr[:, ls]
        cin = jnp.zeros((LRU_SEG, LANES), F32)
        for sgm in range(LRU_SEG):
            cin = jnp.where(seg_id == sgm, carry, cin)
            carry = hloc[sgm:sgm + 1, :] + prod[sgm:sgm + 1, :] * carry
        h_scr[:, ls] = carry
        for i in range(seg_len):
            rows = pl.ds(i, LRU_SEG, stride=seg_len)
            b_scr[c, rows, :] = b_scr[c, rows, :] + a_scr[c, rows, :] * cin
        y_ref[:, ls] = (b_scr[c] * jax.nn.gelu(gate_ref[:, ls], approximate=True)).astype(y_ref.dtype)


def lru_core(proj, conv_w, conv_b, wri, starts, b_r, b_i, lam, *, batch, seq, t):
    m = proj.shape[0]
    width = proj.shape[1] // 2
    nt = seq // t
    assert seq % t == 0 and t % LRU_SEG == 0
    vec = lambda: pl.BlockSpec((1, width), lambda b, s: (0, 0))
    kern = functools.partial(_lru_kernel, starts=tuple(starts), t=t)
    return pl.pallas_call(
        kern, grid=(batch, nt),
        in_specs=[pl.BlockSpec((t, width), lambda b, s: (b * nt + s, 0)),
                  pl.BlockSpec((t, width), lambda b, s: (b * nt + s, 1)),
                  pl.BlockSpec(conv_w.shape, lambda b, s: (0, 0)),
                  vec(),
                  pl.BlockSpec(wri.shape, lambda b, s: (0, 0, 0)),
                  vec(), vec(), vec()],
        out_specs=pl.BlockSpec((t, width), lambda b, s: (b * nt + s, 0)),
        out_shape=jax.ShapeDtypeStruct((m, width), BF16),
        scratch_shapes=[pltpu.VMEM((t + LRU_HALO, width), F32),
                        pltpu.VMEM((width // LANES, t, LANES), F32),
                        pltpu.VMEM((width // LANES, t, LANES), F32),
                        pltpu.VMEM((1, width), F32)],
        compiler_params=_params(("parallel", "arbitrary")),
        name="lru_core",
    )(proj, proj, conv_w, conv_b.reshape(1, width), wri, b_r.reshape(1, width),
      b_i.reshape(1, width), lam.reshape(1, width))


def _col_to_row(col, eye):
    return jnp.sum(jnp.where(eye, col, 0.0), axis=0, keepdims=True)


def _mlstm_kernel(q_ref, k_ref, v_ref, o_ref, g_ref, bif_ref, hg_ref, y_ref,
                  c_scr, n_scr, m_scr, *, nheads, kscale):
    head = pl.program_id(1)
    chunk = pl.program_id(2)
    length = q_ref.shape[0]

    @pl.when(chunk == 0)
    def _():
        c_scr[...] = jnp.zeros_like(c_scr)
        n_scr[...] = jnp.zeros_like(n_scr)
        m_scr[...] = jnp.full_like(m_scr, MLSTM_M_INIT)

    row = lax.broadcasted_iota(jnp.int32, (length, length), 0)
    col = lax.broadcasted_iota(jnp.int32, (length, length), 1)
    eye = row == col
    causal = col <= row

    g = g_ref[...] + bif_ref[...]
    lane = lax.broadcasted_iota(jnp.int32, g.shape, 1)
    i_col = jnp.sum(jnp.where(lane == head, g, 0.0), axis=-1, keepdims=True)
    f_col = jnp.sum(jnp.where(lane == nheads + head, g, 0.0), axis=-1, keepdims=True)
    lf_col = _log_sigmoid(f_col)
    lf_row = _col_to_row(lf_col, eye)
    i_row = _col_to_row(i_col, eye)
    b_col = jnp.sum(jnp.where(causal, lf_row, 0.0), axis=-1, keepdims=True)
    b_row = _col_to_row(b_col, eye)
    gsum = jnp.sum(lf_row, axis=-1, keepdims=True)

    m_prev = m_scr[...]
    dm = jnp.where(causal, b_col - b_row + i_row, NEG)
    inter = b_col + m_prev
    m_t = jnp.maximum(inter, jnp.max(dm, axis=-1, keepdims=True))
    w_intra = jnp.exp(dm - m_t)
    w_inter = jnp.exp(inter - m_t)

    q = q_ref[...]
    ks = k_ref[...] * kscale
    v = v_ref[...]
    s = lax.dot_general(q, ks, (((1,), (1,)), ((), ())), preferred_element_type=F32) * w_intra
    num = (w_inter * jnp.dot(q, c_scr[...].astype(BF16), preferred_element_type=F32)
           + jnp.dot(s.astype(BF16), v, preferred_element_type=F32))
    qn = jnp.sum(q.astype(F32) * n_scr[...], axis=-1, keepdims=True)
    den = w_inter * qn + jnp.sum(s, axis=-1, keepdims=True)
    hval = num / jnp.maximum(jnp.abs(den), jnp.exp(-m_t))

    hn = hval * _rms_scale(hval) * hg_ref[...]
    y_ref[...] = (jax.nn.sigmoid(o_ref[...].astype(F32)) * hn).astype(y_ref.dtype)

    wk_col = gsum - b_col + i_col
    m_new = jnp.maximum(gsum + m_prev, jnp.max(wk_col, axis=0, keepdims=True))
    decay = jnp.exp(gsum + m_prev - m_new)
    kw = jnp.exp(wk_col - m_new) * ks.astype(F32)
    c_scr[...] = decay * c_scr[...] + lax.dot_general(
        kw.astype(BF16), v, (((0,), (0,)), ((), ())), preferred_element_type=F32)
    n_scr[...] = decay * n_scr[...] + jnp.sum(kw, axis=0, keepdims=True)
    m_scr[...] = m_new


def mlstm_core(proj, gates, b_if, head_g, *, batch, seq, chunk):
    m = proj.shape[0]
    d = head_g.shape[0]
    nh = MLSTM_HEADS
    dk, dv = d // 2 // nh, d // nh
    nc = seq // chunk
    ng = gates.shape[1]
    bif = jnp.zeros((1, ng), F32).at[0, :2 * nh].set(b_if.reshape(-1))
    kern = functools.partial(_mlstm_kernel, nheads=nh, kscale=dk ** -0.5)
    kblk0, vblk0, oblk0 = (nh * dk) // dk, (2 * nh * dk) // dv, (2 * nh * dk + d) // dv
    return pl.pallas_call(
        kern, grid=(batch, nh, nc),
        in_specs=[pl.BlockSpec((chunk, dk), lambda b, h, c: (b * nc + c, h)),
                  pl.BlockSpec((chunk, dk), lambda b, h, c: (b * nc + c, kblk0 + h)),
                  pl.BlockSpec((chunk, dv), lambda b, h, c: (b * nc + c, vblk0 + h)),
                  pl.BlockSpec((chunk, dv), lambda b, h, c: (b * nc + c, oblk0 + h)),
                  pl.BlockSpec((chunk, ng), lambda b, h, c: (b * nc + c, 0)),
                  pl.BlockSpec((1, ng), lambda b, h, c: (0, 0)),
                  pl.BlockSpec((1, dv), lambda b, h, c: (0, h))],
        out_specs=pl.BlockSpec((chunk, dv), lambda b, h, c: (b * nc + c, h)),
        out_shape=jax.ShapeDtypeStruct((m, d), BF16),
        scratch_shapes=[pltpu.VMEM((dk, dv), F32), pltpu.VMEM((1, dk), F32), pltpu.VMEM((1, 1), F32)],
        compiler_params=_params(("parallel", "parallel", "arbitrary")),
        name="mlstm_core",
    )(proj, proj, proj, proj, gates, bif, head_g.reshape(1, d))


def _pad_cols(w, n):
    return jnp.pad(w, ((0, 0), (0, n - w.shape[1])))


def kernel(x, norm_g, mlp_w1, mlp_w2, fox_w_in, fox_b_f, fox_w_out, lru_w_in, lru_conv_w, lru_conv_b,
           lru_w_r, lru_b_r, lru_w_i, lru_b_i, lru_lambda, lru_w_out, sb_w_in, sb_w_out,
           mlstm_w_in, mlstm_b_if, mlstm_head_g, mlstm_w_out):
    batch, seq, d = x.shape
    depth = norm_g.shape[0]
    n_mixers = 4
    xf = x.reshape(batch * seq, d)
    tm_proj = 1024
    tm_out = 512
    attn_blk = min(512, seq)
    q_scale = HEAD_DIM ** -0.5 * LOG2E

    for i in range(depth):
        mixer, j = i % n_mixers, i // n_mixers
        if mixer == 0:
            w = fox_w_in[j]
            proj, fgate = norm_proj(xf, norm_g[i, 0], w[:, :3 * d].astype(BF16),
                                    _pad_cols(w[:, 3 * d:], LANES).astype(BF16),
                                    out_dtype=BF16, tm=tm_proj, tn=1024, q_scale=q_scale, q_cols=d)
            f_t = fgate[:, :FOX_HEADS].reshape(batch, seq, FOX_HEADS).transpose(0, 2, 1)
            cum = fox_cum(f_t, fox_b_f[j]).reshape(batch, FOX_HEADS, 1, seq)
            mixed = fox_attention(proj, cum, batch=batch, seq=seq, blk=attn_blk)
            w_out = fox_w_out[j]
        elif mixer == 1:
            width = lru_w_in.shape[2] // 2
            proj = norm_proj(xf, norm_g[i, 0], lru_w_in[j].astype(BF16), out_dtype=F32,
                             tm=tm_proj, tn=width // 3)
            wri, starts = _pack_lru_gates(lru_w_r[j], lru_w_i[j], width)
            mixed = lru_core(proj, lru_conv_w[j], lru_conv_b[j], wri, starts, lru_b_r[j], lru_b_i[j],
                             lru_lambda[j], batch=batch, seq=seq, t=min(256, seq))
            w_out = lru_w_out[j]
        elif mixer == 2:
            proj = norm_proj(xf, norm_g[i, 0], sb_w_in[j].astype(BF16), out_dtype=BF16,
                             tm=tm_proj, tn=1024, q_scale=q_scale, q_cols=d)
            mixed = sb_attention(proj, batch=batch, seq=seq, blk=attn_blk, sub=256)
            w_out = sb_w_out[j]
        else:
            w = mlstm_w_in[j]
            proj, gates = norm_proj(xf, norm_g[i, 0], w[:, :3 * d].astype(BF16),
                                    _pad_cols(w[:, 3 * d:], LANES).astype(BF16),
                                    out_dtype=BF16, tm=tm_proj, tn=1024)
            mixed = mlstm_core(proj, gates, mlstm_b_if[j], mlstm_head_g[j],
                               batch=batch, seq=seq, chunk=min(256, seq))
            w_out = mlstm_w_out[j]
        xf = proj_norm_res(mixed, w_out.astype(BF16), norm_g[i, 1], xf, tm=tm_out)
        xf = mlp_block(xf, norm_g[i, 2], mlp_w1[i].astype(BF16), mlp_w2[i].astype(BF16),
                       norm_g[i, 3], tm=512, tf=1024)
    return xf.reshape(batch, seq, d)
```

```python
import functools
import math

import numpy as np
import jax
import jax.numpy as jnp
from jax import lax
from jax.experimental import pallas as pl
from jax.experimental.pallas import tpu as pltpu

F32 = jnp.float32
BF16 = jnp.bfloat16

EPS = 1e-6
LANES = 128
SUBLANES = 8
VMEM_LIMIT = 56 * 1024 * 1024
NEG = -1e30
LOG2E = math.log2(math.e)

FOX_HEADS = 16
SB_HEADS = 16
HEAD_DIM = 128
LRU_BLOCKS = 16
LRU_C = 8.0
MLSTM_HEADS = 4
MLSTM_M_INIT = -1e30


def _params(sem, vmem=VMEM_LIMIT):
    return pltpu.CompilerParams(dimension_semantics=sem, vmem_limit_bytes=vmem)


def _log_sigmoid(x):
    return jnp.minimum(x, 0.0) - jnp.log1p(jnp.exp(-jnp.abs(x)))


def _rms_scale(x):
    return lax.rsqrt(jnp.mean(x * x, axis=-1, keepdims=True) + EPS)


def _project(h_scr, w_ref, o_ref, q_scale, q_tiles):
    acc = jnp.dot(h_scr[...], w_ref[...], preferred_element_type=F32)
    if q_tiles:
        acc = acc * jnp.where(pl.program_id(1) < q_tiles, q_scale, 1.0)
    o_ref[...] = acc.astype(o_ref.dtype)


def _norm_proj_kernel(x_ref, g_ref, w_ref, o_ref, h_scr, *, q_scale, q_tiles):
    @pl.when(pl.program_id(1) == 0)
    def _():
        x = x_ref[...]
        h_scr[...] = (x * _rms_scale(x) * g_ref[...]).astype(BF16)

    _project(h_scr, w_ref, o_ref, q_scale, q_tiles)


def _norm_proj_gate_kernel(x_ref, g_ref, w_ref, wg_ref, o_ref, og_ref, h_scr, *, q_scale, q_tiles):
    @pl.when(pl.program_id(1) == 0)
    def _():
        x = x_ref[...]
        h = (x * _rms_scale(x) * g_ref[...]).astype(BF16)
        h_scr[...] = h
        og_ref[...] = jnp.dot(h, wg_ref[...], preferred_element_type=F32)

    _project(h_scr, w_ref, o_ref, q_scale, q_tiles)


def norm_proj(x, g, w, wg=None, *, out_dtype, tm, tn, q_scale=1.0, q_cols=0):
    m, d = x.shape
    n = w.shape[1]
    assert m % tm == 0 and n % tn == 0 and q_cols % tn == 0
    grid = (m // tm, n // tn)
    x_spec = pl.BlockSpec((tm, d), lambda i, j: (i, 0))
    g_spec = pl.BlockSpec((1, d), lambda i, j: (0, 0))
    w_spec = pl.BlockSpec((d, tn), lambda i, j: (0, j))
    o_spec = pl.BlockSpec((tm, tn), lambda i, j: (i, j))
    scratch = [pltpu.VMEM((tm, d), BF16)]
    statics = dict(q_scale=q_scale, q_tiles=q_cols // tn)
    if wg is None:
        return pl.pallas_call(
            functools.partial(_norm_proj_kernel, **statics), grid=grid,
            in_specs=[x_spec, g_spec, w_spec], out_specs=o_spec,
            out_shape=jax.ShapeDtypeStruct((m, n), out_dtype),
            scratch_shapes=scratch,
            compiler_params=_params(("parallel", "arbitrary")),
            name="norm_proj",
        )(x, g.reshape(1, d), w)
    ng = wg.shape[1]
    return pl.pallas_call(
        functools.partial(_norm_proj_gate_kernel, **statics), grid=grid,
        in_specs=[x_spec, g_spec, w_spec, pl.BlockSpec((d, ng), lambda i, j: (0, 0))],
        out_specs=[o_spec, pl.BlockSpec((tm, ng), lambda i, j: (i, 0))],
        out_shape=[jax.ShapeDtypeStruct((m, n), out_dtype), jax.ShapeDtypeStruct((m, ng), F32)],
        scratch_shapes=scratch,
        compiler_params=_params(("parallel", "arbitrary")),
        name="norm_proj_gate",
    )(x, g.reshape(1, d), w, wg)


def _proj_norm_res_kernel(a_ref, w_ref, g_ref, x_ref, o_ref):
    y = jnp.dot(a_ref[...], w_ref[...], preferred_element_type=F32)
    o_ref[...] = x_ref[...] + y * _rms_scale(y) * g_ref[...]


def proj_norm_res(a, w, g, x, *, tm):
    m, k = a.shape
    d = w.shape[1]
    assert m % tm == 0
    return pl.pallas_call(
        _proj_norm_res_kernel, grid=(m // tm,),
        in_specs=[pl.BlockSpec((tm, k), lambda i: (i, 0)),
                  pl.BlockSpec((k, d), lambda i: (0, 0)),
                  pl.BlockSpec((1, d), lambda i: (0, 0)),
                  pl.BlockSpec((tm, d), lambda i: (i, 0))],
        out_specs=pl.BlockSpec((tm, d), lambda i: (i, 0)),
        out_shape=jax.ShapeDtypeStruct((m, d), F32),
        compiler_params=_params(("parallel",)),
        name="proj_norm_res",
    )(a, w, g.reshape(1, d), x)


def _mlp_kernel(x_ref, g2_ref, w1_ref, w2_ref, g3_ref, o_ref, h_scr, acc_scr):
    f = pl.program_id(1)

    @pl.when(f == 0)
    def _():
        x = x_ref[...]
        h_scr[...] = (x * _rms_scale(x) * g2_ref[...]).astype(BF16)
        acc_scr[...] = jnp.zeros_like(acc_scr)

    a = jnp.dot(h_scr[...], w1_ref[...], preferred_element_type=F32)
    a = jnp.square(jnp.maximum(a, 0.0)).astype(BF16)
    acc_scr[...] += jnp.dot(a, w2_ref[...], preferred_element_type=F32)

    @pl.when(f == pl.num_programs(1) - 1)
    def _():
        y = acc_scr[...]
        o_ref[...] = x_ref[...] + y * _rms_scale(y) * g3_ref[...]


def mlp_block(x, g2, w1, w2, g3, *, tm, tf):
    m, d = x.shape
    ff = w1.shape[1]
    assert m % tm == 0 and ff % tf == 0
    return pl.pallas_call(
        _mlp_kernel, grid=(m // tm, ff // tf),
        in_specs=[pl.BlockSpec((tm, d), lambda i, f: (i, 0)),
                  pl.BlockSpec((1, d), lambda i, f: (0, 0)),
                  pl.BlockSpec((d, tf), lambda i, f: (0, f)),
                  pl.BlockSpec((tf, d), lambda i, f: (f, 0)),
                  pl.BlockSpec((1, d), lambda i, f: (0, 0))],
        out_specs=pl.BlockSpec((tm, d), lambda i, f: (i, 0)),
        out_shape=jax.ShapeDtypeStruct((m, d), F32),
        scratch_shapes=[pltpu.VMEM((tm, d), BF16), pltpu.VMEM((tm, d), F32)],
        compiler_params=_params(("parallel", "arbitrary")),
        name="mlp_block",
    )(x, g2.reshape(1, d), w1, w2, g3.reshape(1, d))


def _fox_cum_kernel(f_ref, b_ref, c_ref, carry_scr):
    nblk = f_ref.shape[1] // LANES
    lane = lax.broadcasted_iota(jnp.int32, (f_ref.shape[0], LANES), 1)
    carry_scr[...] = jnp.zeros_like(carry_scr)

    def body(j, _):
        off = pl.multiple_of(j * LANES, LANES)
        v = _log_sigmoid(f_ref[:, pl.ds(off, LANES)] + b_ref[...])
        d = 1
        while d < LANES:
            v = v + jnp.where(lane >= d, pltpu.roll(v, d, 1), 0.0)
            d *= 2
        v = v + carry_scr[...]
        c_ref[:, pl.ds(off, LANES)] = v * LOG2E
        carry_scr[...] = v[:, LANES - 1:LANES]
        return 0

    lax.fori_loop(0, nblk, body, 0)


def fox_cum(f_t, b_f):
    b, h, s = f_t.shape
    return pl.pallas_call(
        _fox_cum_kernel, grid=(b,),
        in_specs=[pl.BlockSpec((None, h, s), lambda i: (i, 0, 0)),
                  pl.BlockSpec((h, 1), lambda i: (0, 0))],
        out_specs=pl.BlockSpec((None, h, s), lambda i: (i, 0, 0)),
        out_shape=jax.ShapeDtypeStruct((b, h, s), F32),
        scratch_shapes=[pltpu.VMEM((h, 1), F32)],
        compiler_params=_params(("parallel",)),
        name="fox_cum",
    )(f_t, b_f.reshape(h, 1))


def _rep(x, n):
    return jnp.concatenate([x] * n, axis=1) if n > 1 else x


def _fox_attn_kernel(q_ref, k_ref, v_ref, c_ref, o_ref, m_scr, l_scr, acc_scr, s_scr, p_scr, *, blk):
    qi = pl.program_id(2)
    q = q_ref[...]
    q0 = pl.multiple_of(qi * blk, blk)
    c0 = c_ref[:, pl.ds(q0, LANES)][:, 0:1]

    def logits(kstart):
        return lax.dot_general(q, k_ref[pl.ds(kstart, blk), :], (((1,), (1,)), ((), ())),
                               preferred_element_type=F32)

    def weighted_values(kstart):
        return jnp.dot(p_scr[...], v_ref[pl.ds(kstart, blk), :], preferred_element_type=F32)

    def biased(kstart):
        return s_scr[...] + (c0 - c_ref[:, pl.ds(kstart, blk)])

    def softmax_update(s, pv_prev):
        m_prev = m_scr[...]
        m_new = jnp.maximum(m_prev, jnp.max(s, axis=-1, keepdims=True))
        p = jnp.exp2(s - _rep(m_new, blk // LANES))
        alpha = jnp.exp2(m_prev - m_new)
        l_scr[...] = alpha * l_scr[...] + jnp.sum(p, axis=-1, keepdims=True)
        acc_scr[...] = alpha * (acc_scr[...] + pv_prev)
        m_scr[...] = m_new
        p_scr[...] = p.astype(BF16)

    def block_start(j):
        return pl.multiple_of(jnp.maximum(j, 0) * blk, blk)

    m_scr[...] = jnp.full_like(m_scr, NEG)
    l_scr[...] = jnp.zeros_like(l_scr)
    acc_scr[...] = jnp.zeros_like(acc_scr)
    p_scr[...] = jnp.zeros_like(p_scr)
    s_scr[...] = logits(0)

    def body(j, carry):
        pv_prev = weighted_values(block_start(j - 1))
        s = biased(block_start(j))
        s_scr[...] = logits(block_start(j + 1))
        softmax_update(s, pv_prev)
        return carry

    lax.fori_loop(0, qi, body, 0)
    pv_prev = weighted_values(block_start(qi - 1))
    s = biased(q0)
    row = lax.broadcasted_iota(jnp.int32, s.shape, 0)
    col = lax.broadcasted_iota(jnp.int32, s.shape, 1)
    softmax_update(jnp.where(col <= row, s, NEG), pv_prev)
    o_ref[...] = ((acc_scr[...] + weighted_values(q0)) / l_scr[...]).astype(o_ref.dtype)


def fox_attention(proj, cum, *, batch, seq, blk):
    m = proj.shape[0]
    h, dh = FOX_HEADS, HEAD_DIM
    assert dh == LANES and seq % blk == 0
    nblk = seq // blk
    kern = functools.partial(_fox_attn_kernel, blk=blk)
    return pl.pallas_call(
        kern, grid=(batch, h, nblk),
        in_specs=[
            pl.BlockSpec((blk, dh), lambda b, hh, i: (b * nblk + i, hh)),
            pl.BlockSpec((seq, dh), lambda b, hh, i: (b, h + hh)),
            pl.BlockSpec((seq, dh), lambda b, hh, i: (b, 2 * h + hh)),
            pl.BlockSpec((None, None, 1, seq), lambda b, hh, i: (b, hh, 0, 0)),
        ],
        out_specs=pl.BlockSpec((blk, dh), lambda b, hh, i: (b * nblk + i, hh)),
        out_shape=jax.ShapeDtypeStruct((m, h * dh), BF16),
        scratch_shapes=[pltpu.VMEM((blk, LANES), F32), pltpu.VMEM((blk, LANES), F32),
                        pltpu.VMEM((blk, dh), F32), pltpu.VMEM((blk, blk), F32),
                        pltpu.VMEM((blk, blk), BF16)],
        compiler_params=_params(("parallel", "parallel", "arbitrary")),
        name="fox_attention",
    )(proj, proj, proj, cum)


SB_LOGIT_CAP = 126.0


def _sb_attn_kernel(q_ref, k_ref, v_ref, u_ref, o_ref, carry_scr, acc_scr, z_scr, a_scr, *, blk, sub):
    qi = pl.program_id(2)
    q = q_ref[...]
    q0 = pl.multiple_of(qi * blk, blk)
    nsub = blk // sub

    def block_start(n):
        return pl.multiple_of(jnp.maximum(qi - n, 0) * blk, blk)

    def logits(kstart):
        return lax.dot_general(q, k_ref[pl.ds(kstart, blk), :], (((1,), (1,)), ((), ())),
                               preferred_element_type=F32)

    def weighted_values(kstart):
        return jnp.dot(a_scr[...], v_ref[pl.ds(kstart, blk), :], preferred_element_type=F32)

    def weights(z, strict):
        z = jnp.minimum(z, SB_LOGIT_CAP)
        sp = jnp.log(1.0 + jnp.exp2(z)) * LOG2E
        if strict is not None:
            sp = jnp.where(strict, sp, 0.0)
        carry = carry_scr[...]
        for c in range(nsub - 1, -1, -1):
            sl = slice(c * sub, (c + 1) * sub)
            sp_c = sp[:, sl]
            hi = sp_c.astype(BF16)
            lo = (sp_c - hi.astype(F32)).astype(BF16)
            g = jnp.dot(jnp.concatenate([hi, lo], axis=1), u_ref[...], preferred_element_type=F32)
            a_c = jnp.exp2(z[:, sl] - g - _rep(carry, sub // LANES))
            if strict is not None:
                a_c = jnp.where(strict[:, sl], a_c, 0.0)
            a_scr[:, sl] = a_c.astype(BF16)
            carry = carry + jnp.sum(sp_c, axis=-1, keepdims=True)
        carry_scr[...] = carry

    carry_scr[...] = jnp.zeros_like(carry_scr)
    acc_scr[...] = jnp.zeros_like(acc_scr)
    row = lax.broadcasted_iota(jnp.int32, (blk, blk), 0)
    col = lax.broadcasted_iota(jnp.int32, (blk, blk), 1)
    weights(logits(q0), col < row)
    z_scr[...] = logits(block_start(1))

    def body(n, c):
        acc_scr[...] += weighted_values(block_start(n - 1))
        z = z_scr[...]
        z_scr[...] = logits(block_start(n + 1))
        weights(z, None)
        return c

    lax.fori_loop(1, qi + 1, body, 0)
    o_ref[...] = (acc_scr[...] + weighted_values(block_start(qi))).astype(o_ref.dtype)


def sb_attention(proj, *, batch, seq, blk, sub):
    m = proj.shape[0]
    h, dh = SB_HEADS, HEAD_DIM
    assert seq % blk == 0 and blk % sub == 0
    nblk = seq // blk
    tri = np.tril(np.ones((sub, sub), np.float32))
    u = jnp.asarray(np.concatenate([tri, tri], axis=0), BF16)
    kern = functools.partial(_sb_attn_kernel, blk=blk, sub=sub)
    return pl.pallas_call(
        kern, grid=(batch, h, nblk),
        in_specs=[
            pl.BlockSpec((blk, dh), lambda b, hh, i: (b * nblk + i, hh)),
            pl.BlockSpec((seq, dh), lambda b, hh, i: (b, h + hh)),
            pl.BlockSpec((seq, dh), lambda b, hh, i: (b, 2 * h + hh)),
            pl.BlockSpec((2 * sub, sub), lambda b, hh, i: (0, 0)),
        ],
        out_specs=pl.BlockSpec((blk, dh), lambda b, hh, i: (b * nblk + i, hh)),
        out_shape=jax.ShapeDtypeStruct((m, h * dh), BF16),
        scratch_shapes=[pltpu.VMEM((blk, LANES), F32), pltpu.VMEM((blk, dh), F32),
                        pltpu.VMEM((blk, blk), F32), pltpu.VMEM((blk, blk), BF16)],
        compiler_params=_params(("parallel", "parallel", "arbitrary")),
        name="sb_attention",
    )(proj, proj, proj, u)


LRU_TN = 3 * LANES
LRU_KW = 7 * LANES
LRU_HALO = 8


def _lru_windows(width, bd):
    starts = []
    for j in range(width // LRU_TN):
        c0, c1 = j * LRU_TN, (j + 1) * LRU_TN - 1
        r0, r1 = (c0 // bd) * bd, (c1 // bd + 1) * bd
        k0 = min((r0 // LANES) * LANES, width - LRU_KW)
        assert k0 <= r0 and r1 <= k0 + LRU_KW
        starts.append(k0)
    return starts


def _pack_lru_gates(w_r, w_i, width):
    nb, bd, _ = w_r.shape
    starts = _lru_windows(width, bd)
    eye = jnp.eye(nb, dtype=w_r.dtype)
    dense_r = jnp.einsum('nde,nm->ndme', w_r, eye).reshape(width, width)
    dense_i = jnp.einsum('nde,nm->ndme', w_i, eye).reshape(width, width)
    tiles = []
    for j, k0 in enumerate(starts):
        cs = slice(j * LRU_TN, (j + 1) * LRU_TN)
        tiles.append(jnp.concatenate([dense_r[k0:k0 + LRU_KW, cs], dense_i[k0:k0 + LRU_KW, cs]], axis=1))
    return jnp.stack(tiles).astype(BF16), starts


def _lru_kernel(gate_ref, u_ref, cw_ref, cb_ref, wri_ref, br_ref, bi_ref, lam_ref, y_ref,
                ubuf, a_scr, b_scr, h_scr, *, starts, t):
    sblk = pl.program_id(1)
    width = u_ref.shape[1]

    @pl.when(sblk == 0)
    def _():
        ubuf[0:LRU_HALO, :] = jnp.zeros((LRU_HALO, width), F32)
        h_scr[...] = jnp.zeros_like(h_scr)

    ubuf[LRU_HALO:LRU_HALO + t, :] = u_ref[...]
    nconv = cw_ref.shape[0]
    uc = cb_ref[...] + cw_ref[nconv - 1:nconv, :] * u_ref[...]
    for j in range(nconv - 1):
        off = LRU_HALO - (nconv - 1) + j
        uc = uc + cw_ref[j:j + 1, :] * ubuf[off:off + t, :]
    ubuf[0:LRU_HALO, :] = u_ref[t - LRU_HALO:t, :]

    ucb = uc.astype(BF16)
    log_sig_lam = _log_sigmoid(lam_ref[...])
    row_in_group = lax.broadcasted_iota(jnp.int32, (t, LRU_TN), 0) % SUBLANES
    for j, k0 in enumerate(starts):
        cs = slice(j * LRU_TN, (j + 1) * LRU_TN)
        pre = jnp.dot(ucb[:, k0:k0 + LRU_KW], wri_ref[j], preferred_element_type=F32)
        r = jax.nn.sigmoid(pre[:, :LRU_TN] + br_ref[:, cs])
        ig = jax.nn.sigmoid(pre[:, LRU_TN:] + bi_ref[:, cs])
        log_a = LRU_C * r * log_sig_lam[:, cs]
        a = jnp.exp(log_a)
        bterm = jnp.sqrt(1.0 - a * a) * (ig * uc[:, cs])
        d = 1
        while d < SUBLANES:
            take = row_in_group >= d
            b_sh = pltpu.roll(bterm, d, 0)
            a_sh = pltpu.roll(a, d, 0)
            bterm = jnp.where(take, a * b_sh + bterm, bterm)
            a = jnp.where(take, a * a_sh, a)
            d *= 2
        a_scr[:, cs] = a
        b_scr[:, cs] = bterm

    h_in = h_scr[...]
    for r in range(t // SUBLANES):
        rows = slice(r * SUBLANES, (r + 1) * SUBLANES)
        h = b_scr[rows, :] + a_scr[rows, :] * h_in
        b_scr[rows, :] = h
        h_in = h[SUBLANES - 1:SUBLANES, :]
    h_scr[...] = h_in
    y_ref[...] = (b_scr[...] * jax.nn.gelu(gate_ref[...], approximate=True)).astype(y_ref.dtype)


def lru_core(proj, conv_w, conv_b, wri, starts, b_r, b_i, lam, *, batch, seq, t):
    m = proj.shape[0]
    width = proj.shape[1] // 2
    nt = seq // t
    assert seq % t == 0 and t % SUBLANES == 0
    vec = lambda: pl.BlockSpec((1, width), lambda b, s: (0, 0))
    kern = functools.partial(_lru_kernel, starts=tuple(starts), t=t)
    return pl.pallas_call(
        kern, grid=(batch, nt),
        in_specs=[pl.BlockSpec((t, width), lambda b, s: (b * nt + s, 0)),
                  pl.BlockSpec((t, width), lambda b, s: (b * nt + s, 1)),
                  pl.BlockSpec(conv_w.shape, lambda b, s: (0, 0)),
                  vec(),
                  pl.BlockSpec(wri.shape, lambda b, s: (0, 0, 0)),
                  vec(), vec(), vec()],
        out_specs=pl.BlockSpec((t, width), lambda b, s: (b * nt + s, 0)),
        out_shape=jax.ShapeDtypeStruct((m, width), BF16),
        scratch_shapes=[pltpu.VMEM((t + LRU_HALO, width), F32),
                        pltpu.VMEM((t, width), F32),
                        pltpu.VMEM((t, width), F32),
                        pltpu.VMEM((1, width), F32)],
        compiler_params=_params(("parallel", "arbitrary")),
        name="lru_core",
    )(proj, proj, conv_w, conv_b.reshape(1, width), wri, b_r.reshape(1, width),
      b_i.reshape(1, width), lam.reshape(1, width))


def _col_to_row(col, eye):
    return jnp.sum(jnp.where(eye, col, 0.0), axis=0, keepdims=True)


def _mlstm_kernel(q_ref, k_ref, v_ref, o_ref, g_ref, bif_ref, hg_ref, y_ref,
                  c_scr, n_scr, m_scr, *, nheads, kscale):
    head = pl.program_id(1)
    chunk = pl.program_id(2)
    length = q_ref.shape[0]

    @pl.when(chunk == 0)
    def _():
        c_scr[...] = jnp.zeros_like(c_scr)
        n_scr[...] = jnp.zeros_like(n_scr)
        m_scr[...] = jnp.full_like(m_scr, MLSTM_M_INIT)

    row = lax.broadcasted_iota(jnp.int32, (length, length), 0)
    col = lax.broadcasted_iota(jnp.int32, (length, length), 1)
    eye = row == col
    causal = col <= row

    g = g_ref[...] + bif_ref[...]
    lane = lax.broadcasted_iota(jnp.int32, g.shape, 1)
    i_col = jnp.sum(jnp.where(lane == head, g, 0.0), axis=-1, keepdims=True)
    f_col = jnp.sum(jnp.where(lane == nheads + head, g, 0.0), axis=-1, keepdims=True)
    lf_col = _log_sigmoid(f_col)
    lf_row = _col_to_row(lf_col, eye)
    i_row = _col_to_row(i_col, eye)
    b_col = jnp.sum(jnp.where(causal, lf_row, 0.0), axis=-1, keepdims=True)
    b_row = _col_to_row(b_col, eye)
    gsum = jnp.sum(lf_row, axis=-1, keepdims=True)

    m_prev = m_scr[...]
    dm = jnp.where(causal, b_col - b_row + i_row, NEG)
    inter = b_col + m_prev
    m_t = jnp.maximum(inter, jnp.max(dm, axis=-1, keepdims=True))
    w_intra = jnp.exp(dm - m_t)
    w_inter = jnp.exp(inter - m_t)

    q = q_ref[...]
    ks = k_ref[...] * kscale
    v = v_ref[...]
    s = lax.dot_general(q, ks, (((1,), (1,)), ((), ())), preferred_element_type=F32) * w_intra
    num = (w_inter * jnp.dot(q, c_scr[...].astype(BF16), preferred_element_type=F32)
           + jnp.dot(s.astype(BF16), v, preferred_element_type=F32))
    qn = jnp.sum(q.astype(F32) * n_scr[...], axis=-1, keepdims=True)
    den = w_inter * qn + jnp.sum(s, axis=-1, keepdims=True)
    hval = num / jnp.maximum(jnp.abs(den), jnp.exp(-m_t))

    hn = hval * _rms_scale(hval) * hg_ref[...]
    y_ref[...] = (jax.nn.sigmoid(o_ref[...].astype(F32)) * hn).astype(y_ref.dtype)

    wk_col = gsum - b_col + i_col
    m_new = jnp.maximum(gsum + m_prev, jnp.max(wk_col, axis=0, keepdims=True))
    decay = jnp.exp(gsum + m_prev - m_new)
    kw = jnp.exp(wk_col - m_new) * ks.astype(F32)
    c_scr[...] = decay * c_scr[...] + lax.dot_general(
        kw.astype(BF16), v, (((0,), (0,)), ((), ())), preferred_element_type=F32)
    n_scr[...] = decay * n_scr[...] + jnp.sum(kw, axis=0, keepdims=True)
    m_scr[...] = m_new


def mlstm_core(proj, gates, b_if, head_g, *, batch, seq, chunk):
    m = proj.shape[0]
    d = head_g.shape[0]
    nh = MLSTM_HEADS
    dk, dv = d // 2 // nh, d // nh
    nc = seq // chunk
    ng = gates.shape[1]
    bif = jnp.zeros((1, ng), F32).at[0, :2 * nh].set(b_if.reshape(-1))
    kern = functools.partial(_mlstm_kernel, nheads=nh, kscale=dk ** -0.5)
    kblk0, vblk0, oblk0 = (nh * dk) // dk, (2 * nh * dk) // dv, (2 * nh * dk + d) // dv
    return pl.pallas_call(
        kern, grid=(batch, nh, nc),
        in_specs=[pl.BlockSpec((chunk, dk), lambda b, h, c: (b * nc + c, h)),
                  pl.BlockSpec((chunk, dk), lambda b, h, c: (b * nc + c, kblk0 + h)),
                  pl.BlockSpec((chunk, dv), lambda b, h, c: (b * nc + c, vblk0 + h)),
                  pl.BlockSpec((chunk, dv), lambda b, h, c: (b * nc + c, oblk0 + h)),
                  pl.BlockSpec((chunk, ng), lambda b, h, c: (b * nc + c, 0)),
                  pl.BlockSpec((1, ng), lambda b, h, c: (0, 0)),
                  pl.BlockSpec((1, dv), lambda b, h, c: (0, h))],
        out_specs=pl.BlockSpec((chunk, dv), lambda b, h, c: (b * nc + c, h)),
        out_shape=jax.ShapeDtypeStruct((m, d), BF16),
        scratch_shapes=[pltpu.VMEM((dk, dv), F32), pltpu.VMEM((1, dk), F32), pltpu.VMEM((1, 1), F32)],
        compiler_params=_params(("parallel", "parallel", "arbitrary")),
        name="mlstm_core",
    )(proj, proj, proj, proj, gates, bif, head_g.reshape(1, d))


def _pad_cols(w, n):
    return jnp.pad(w, ((0, 0), (0, n - w.shape[1])))


def kernel(x, norm_g, mlp_w1, mlp_w2, fox_w_in, fox_b_f, fox_w_out, lru_w_in, lru_conv_w, lru_conv_b,
           lru_w_r, lru_b_r, lru_w_i, lru_b_i, lru_lambda, lru_w_out, sb_w_in, sb_w_out,
           mlstm_w_in, mlstm_b_if, mlstm_head_g, mlstm_w_out):
    batch, seq, d = x.shape
    depth = norm_g.shape[0]
    n_mixers = 4
    xf = x.reshape(batch * seq, d)
    tm_proj = 1024
    tm_out = 512
    attn_blk = min(512, seq)
    q_scale = HEAD_DIM ** -0.5 * LOG2E

    for i in range(depth):
        mixer, j = i % n_mixers, i // n_mixers
        if mixer == 0:
            w = fox_w_in[j]
            proj, fgate = norm_proj(xf, norm_g[i, 0], w[:, :3 * d].astype(BF16),
                                    _pad_cols(w[:, 3 * d:], LANES).astype(BF16),
                                    out_dtype=BF16, tm=tm_proj, tn=1024, q_scale=q_scale, q_cols=d)
            f_t = fgate[:, :FOX_HEADS].reshape(batch, seq, FOX_HEADS).transpose(0, 2, 1)
            cum = fox_cum(f_t, fox_b_f[j]).reshape(batch, FOX_HEADS, 1, seq)
            mixed = fox_attention(proj, cum, batch=batch, seq=seq, blk=attn_blk)
            w_out = fox_w_out[j]
        elif mixer == 1:
            width = lru_w_in.shape[2] // 2
            proj = norm_proj(xf, norm_g[i, 0], lru_w_in[j].astype(BF16), out_dtype=F32,
                             tm=tm_proj, tn=width // 3)
            wri, starts = _pack_lru_gates(lru_w_r[j], lru_w_i[j], width)
            mixed = lru_core(proj, lru_conv_w[j], lru_conv_b[j], wri, starts, lru_b_r[j], lru_b_i[j],
                             lru_lambda[j], batch=batch, seq=seq, t=min(256, seq))
            w_out = lru_w_out[j]
        elif mixer == 2:
            proj = norm_proj(xf, norm_g[i, 0], sb_w_in[j].astype(BF16), out_dtype=BF16,
                             tm=tm_proj, tn=1024, q_scale=q_scale, q_cols=d)
            mixed = sb_attention(proj, batch=batch, seq=seq, blk=attn_blk, sub=256)
            w_out = sb_w_out[j]
        else:
            w = mlstm_w_in[j]
            proj, gates = norm_proj(xf, norm_g[i, 0], w[:, :3 * d].astype(BF16),
                                    _pad_cols(w[:, 3 * d:], LANES).astype(BF16),
                                    out_dtype=BF16, tm=tm_proj, tn=1024)
            mixed = mlstm_core(proj, gates, mlstm_b_if[j], mlstm_head_g[j],
                               batch=batch, seq=seq, chunk=min(256, seq))
            w_out = mlstm_w_out[j]
        xf = proj_norm_res(mixed, w_out.astype(BF16), norm_g[i, 1], xf, tm=tm_out)
        xf = mlp_block(xf, norm_g[i, 2], mlp_w1[i].astype(BF16), mlp_w2[i].astype(BF16),
                       norm_g[i, 3], tm=512, tf=1024)
    return xf.reshape(batch, seq, d)
```

```python
import functools
import math

import numpy as np
import jax
import jax.numpy as jnp
from jax import lax
from jax.experimental import pallas as pl
from jax.experimental.pallas import tpu as pltpu

F32 = jnp.float32
BF16 = jnp.bfloat16

EPS = 1e-6
LANES = 128
SUBLANES = 8
VMEM_LIMIT = 56 * 1024 * 1024
NEG = -1e30
LOG2E = math.log2(math.e)

FOX_HEADS = 16
SB_HEADS = 16
HEAD_DIM = 128
LRU_BLOCKS = 16
LRU_C = 8.0
MLSTM_HEADS = 4
MLSTM_M_INIT = -1e30


def _params(sem, vmem=VMEM_LIMIT):
    return pltpu.CompilerParams(dimension_semantics=sem, vmem_limit_bytes=vmem)


def _log_sigmoid(x):
    return jnp.minimum(x, 0.0) - jnp.log1p(jnp.exp(-jnp.abs(x)))


def _rms_scale(x):
    return lax.rsqrt(jnp.mean(x * x, axis=-1, keepdims=True) + EPS)


def _project(h_scr, w_ref, o_ref, q_scale, q_tiles):
    acc = jnp.dot(h_scr[...], w_ref[...], preferred_element_type=F32)
    if q_tiles:
        acc = acc * jnp.where(pl.program_id(1) < q_tiles, q_scale, 1.0)
    o_ref[...] = acc.astype(o_ref.dtype)


def _norm_proj_kernel(x_ref, g_ref, w_ref, o_ref, h_scr, *, q_scale, q_tiles):
    @pl.when(pl.program_id(1) == 0)
    def _():
        x = x_ref[...]
        h_scr[...] = (x * _rms_scale(x) * g_ref[...]).astype(BF16)

    _project(h_scr, w_ref, o_ref, q_scale, q_tiles)


def _norm_proj_gate_kernel(x_ref, g_ref, w_ref, wg_ref, o_ref, og_ref, h_scr, *, q_scale, q_tiles):
    @pl.when(pl.program_id(1) == 0)
    def _():
        x = x_ref[...]
        h = (x * _rms_scale(x) * g_ref[...]).astype(BF16)
        h_scr[...] = h
        og_ref[...] = jnp.dot(h, wg_ref[...], preferred_element_type=F32)

    _project(h_scr, w_ref, o_ref, q_scale, q_tiles)


def norm_proj(x, g, w, wg=None, *, out_dtype, tm, tn, q_scale=1.0, q_cols=0):
    m, d = x.shape
    n = w.shape[1]
    assert m % tm == 0 and n % tn == 0 and q_cols % tn == 0
    grid = (m // tm, n // tn)
    x_spec = pl.BlockSpec((tm, d), lambda i, j: (i, 0))
    g_spec = pl.BlockSpec((1, d), lambda i, j: (0, 0))
    w_spec = pl.BlockSpec((d, tn), lambda i, j: (0, j))
    o_spec = pl.BlockSpec((tm, tn), lambda i, j: (i, j))
    scratch = [pltpu.VMEM((tm, d), BF16)]
    statics = dict(q_scale=q_scale, q_tiles=q_cols // tn)
    if wg is None:
        return pl.pallas_call(
            functools.partial(_norm_proj_kernel, **statics), grid=grid,
            in_specs=[x_spec, g_spec, w_spec], out_specs=o_spec,
            out_shape=jax.ShapeDtypeStruct((m, n), out_dtype),
            scratch_shapes=scratch,
            compiler_params=_params(("parallel", "arbitrary")),
            name="norm_proj",
        )(x, g.reshape(1, d), w)
    ng = wg.shape[1]
    return pl.pallas_call(
        functools.partial(_norm_proj_gate_kernel, **statics), grid=grid,
        in_specs=[x_spec, g_spec, w_spec, pl.BlockSpec((d, ng), lambda i, j: (0, 0))],
        out_specs=[o_spec, pl.BlockSpec((tm, ng), lambda i, j: (i, 0))],
        out_shape=[jax.ShapeDtypeStruct((m, n), out_dtype), jax.ShapeDtypeStruct((m, ng), F32)],
        scratch_shapes=scratch,
        compiler_params=_params(("parallel", "arbitrary")),
        name="norm_proj_gate",
    )(x, g.reshape(1, d), w, wg)


def _proj_norm_res_kernel(a_ref, w_ref, g_ref, x_ref, o_ref):
    y = jnp.dot(a_ref[...], w_ref[...], preferred_element_type=F32)
    o_ref[...] = x_ref[...] + y * _rms_scale(y) * g_ref[...]


def proj_norm_res(a, w, g, x, *, tm):
    m, k = a.shape
    d = w.shape[1]
    assert m % tm == 0
    return pl.pallas_call(
        _proj_norm_res_kernel, grid=(m // tm,),
        in_specs=[pl.BlockSpec((tm, k), lambda i: (i, 0)),
                  pl.BlockSpec((k, d), lambda i: (0, 0)),
                  pl.BlockSpec((1, d), lambda i: (0, 0)),
                  pl.BlockSpec((tm, d), lambda i: (i, 0))],
        out_specs=pl.BlockSpec((tm, d), lambda i: (i, 0)),
        out_shape=jax.ShapeDtypeStruct((m, d), F32),
        compiler_params=_params(("parallel",)),
        name="proj_norm_res",
    )(a, w, g.reshape(1, d), x)


def _mlp_kernel(x_ref, g2_ref, w1_ref, w2_ref, g3_ref, o_ref, h_scr, acc_scr):
    f = pl.program_id(1)

    @pl.when(f == 0)
    def _():
        x = x_ref[...]
        h_scr[...] = (x * _rms_scale(x) * g2_ref[...]).astype(BF16)
        acc_scr[...] = jnp.zeros_like(acc_scr)

    a = jnp.dot(h_scr[...], w1_ref[...], preferred_element_type=F32)
    a = jnp.square(jnp.maximum(a, 0.0)).astype(BF16)
    acc_scr[...] += jnp.dot(a, w2_ref[...], preferred_element_type=F32)

    @pl.when(f == pl.num_programs(1) - 1)
    def _():
        y = acc_scr[...]
        o_ref[...] = x_ref[...] + y * _rms_scale(y) * g3_ref[...]


def mlp_block(x, g2, w1, w2, g3, *, tm, tf):
    m, d = x.shape
    ff = w1.shape[1]
    assert m % tm == 0 and ff % tf == 0
    return pl.pallas_call(
        _mlp_kernel, grid=(m // tm, ff // tf),
        in_specs=[pl.BlockSpec((tm, d), lambda i, f: (i, 0)),
                  pl.BlockSpec((1, d), lambda i, f: (0, 0)),
                  pl.BlockSpec((d, tf), lambda i, f: (0, f)),
                  pl.BlockSpec((tf, d), lambda i, f: (f, 0)),
                  pl.BlockSpec((1, d), lambda i, f: (0, 0))],
        out_specs=pl.BlockSpec((tm, d), lambda i, f: (i, 0)),
        out_shape=jax.ShapeDtypeStruct((m, d), F32),
        scratch_shapes=[pltpu.VMEM((tm, d), BF16), pltpu.VMEM((tm, d), F32)],
        compiler_params=_params(("parallel", "arbitrary")),
        name="mlp_block",
    )(x, g2.reshape(1, d), w1, w2, g3.reshape(1, d))


def _fox_cum_kernel(f_ref, b_ref, c_ref, carry_scr):
    nblk = f_ref.shape[1] // LANES
    lane = lax.broadcasted_iota(jnp.int32, (f_ref.shape[0], LANES), 1)
    carry_scr[...] = jnp.zeros_like(carry_scr)

    def body(j, _):
        off = pl.multiple_of(j * LANES, LANES)
        v = _log_sigmoid(f_ref[:, pl.ds(off, LANES)] + b_ref[...])
        d = 1
        while d < LANES:
            v = v + jnp.where(lane >= d, pltpu.roll(v, d, 1), 0.0)
            d *= 2
        v = v + carry_scr[...]
        c_ref[:, pl.ds(off, LANES)] = v * LOG2E
        carry_scr[...] = v[:, LANES - 1:LANES]
        return 0

    lax.fori_loop(0, nblk, body, 0)


def fox_cum(f_t, b_f):
    b, h, s = f_t.shape
    return pl.pallas_call(
        _fox_cum_kernel, grid=(b,),
        in_specs=[pl.BlockSpec((None, h, s), lambda i: (i, 0, 0)),
                  pl.BlockSpec((h, 1), lambda i: (0, 0))],
        out_specs=pl.BlockSpec((None, h, s), lambda i: (i, 0, 0)),
        out_shape=jax.ShapeDtypeStruct((b, h, s), F32),
        scratch_shapes=[pltpu.VMEM((h, 1), F32)],
        compiler_params=_params(("parallel",)),
        name="fox_cum",
    )(f_t, b_f.reshape(h, 1))


def _keys_transposed(proj, d):
    return proj[:, d:2 * d].T


def _rep(x, n):
    return jnp.concatenate([x] * n, axis=1) if n > 1 else x


def _fox_attn_kernel(q_ref, kt_ref, v_ref, c_ref, o_ref, m_scr, l_scr, acc_scr, s_scr, p_scr, *, blk):
    qi = pl.program_id(2)
    q = q_ref[...]
    q0 = pl.multiple_of(qi * blk, blk)
    c0 = c_ref[:, pl.ds(q0, LANES)][:, 0:1]

    def logits(kstart):
        return jnp.dot(q, kt_ref[:, pl.ds(kstart, blk)], preferred_element_type=F32)

    def weighted_values(kstart):
        return jnp.dot(p_scr[...], v_ref[pl.ds(kstart, blk), :], preferred_element_type=F32)

    def biased(kstart):
        return s_scr[...] + (c0 - c_ref[:, pl.ds(kstart, blk)])

    def softmax_update(s, pv_prev):
        m_prev = m_scr[...]
        m_new = jnp.maximum(m_prev, jnp.max(s, axis=-1, keepdims=True))
        p = jnp.exp2(s - _rep(m_new, blk // LANES))
        alpha = jnp.exp2(m_prev - m_new)
        l_scr[...] = alpha * l_scr[...] + jnp.sum(p, axis=-1, keepdims=True)
        acc_scr[...] = alpha * (acc_scr[...] + pv_prev)
        m_scr[...] = m_new
        p_scr[...] = p.astype(BF16)

    def block_start(j):
        return pl.multiple_of(jnp.maximum(j, 0) * blk, blk)

    m_scr[...] = jnp.full_like(m_scr, NEG)
    l_scr[...] = jnp.zeros_like(l_scr)
    acc_scr[...] = jnp.zeros_like(acc_scr)
    p_scr[...] = jnp.zeros_like(p_scr)
    s_scr[...] = logits(0)

    def body(j, carry):
        pv_prev = weighted_values(block_start(j - 1))
        s = biased(block_start(j))
        s_scr[...] = logits(block_start(j + 1))
        softmax_update(s, pv_prev)
        return carry

    lax.fori_loop(0, qi, body, 0)
    pv_prev = weighted_values(block_start(qi - 1))
    s = biased(q0)
    row = lax.broadcasted_iota(jnp.int32, s.shape, 0)
    col = lax.broadcasted_iota(jnp.int32, s.shape, 1)
    softmax_update(jnp.where(col <= row, s, NEG), pv_prev)
    o_ref[...] = ((acc_scr[...] + weighted_values(q0)) / l_scr[...]).astype(o_ref.dtype)


def fox_attention(proj, cum, *, batch, seq, blk):
    m = proj.shape[0]
    h, dh = FOX_HEADS, HEAD_DIM
    assert dh == LANES and seq % blk == 0
    nblk = seq // blk
    kern = functools.partial(_fox_attn_kernel, blk=blk)
    return pl.pallas_call(
        kern, grid=(batch, h, nblk),
        in_specs=[
            pl.BlockSpec((blk, dh), lambda b, hh, i: (b * nblk + i, hh)),
            pl.BlockSpec((dh, seq), lambda b, hh, i: (hh, b)),
            pl.BlockSpec((seq, dh), lambda b, hh, i: (b, 2 * h + hh)),
            pl.BlockSpec((None, None, 1, seq), lambda b, hh, i: (b, hh, 0, 0)),
        ],
        out_specs=pl.BlockSpec((blk, dh), lambda b, hh, i: (b * nblk + i, hh)),
        out_shape=jax.ShapeDtypeStruct((m, h * dh), BF16),
        scratch_shapes=[pltpu.VMEM((blk, LANES), F32), pltpu.VMEM((blk, LANES), F32),
                        pltpu.VMEM((blk, dh), F32), pltpu.VMEM((blk, blk), F32),
                        pltpu.VMEM((blk, blk), BF16)],
        compiler_params=_params(("parallel", "parallel", "arbitrary")),
        name="fox_attention",
    )(proj, _keys_transposed(proj, h * dh), proj, cum)


SB_LOGIT_CAP = 126.0


def _sb_attn_kernel(q_ref, kt_ref, v_ref, u_ref, o_ref, carry_scr, acc_scr, z_scr, a_scr, *, blk, sub):
    qi = pl.program_id(2)
    q = q_ref[...]
    q0 = pl.multiple_of(qi * blk, blk)
    nsub = blk // sub

    def block_start(n):
        return pl.multiple_of(jnp.maximum(qi - n, 0) * blk, blk)

    def logits(kstart):
        return jnp.dot(q, kt_ref[:, pl.ds(kstart, blk)], preferred_element_type=F32)

    def weighted_values(kstart):
        return jnp.dot(a_scr[...], v_ref[pl.ds(kstart, blk), :], preferred_element_type=F32)

    def weights(z, strict):
        z = jnp.minimum(z, SB_LOGIT_CAP)
        sp = jnp.log(1.0 + jnp.exp2(z)) * LOG2E
        if strict is not None:
            sp = jnp.where(strict, sp, 0.0)
        carry = carry_scr[...]
        for c in range(nsub - 1, -1, -1):
            sl = slice(c * sub, (c + 1) * sub)
            sp_c = sp[:, sl]
            hi = sp_c.astype(BF16)
            lo = (sp_c - hi.astype(F32)).astype(BF16)
            g = jnp.dot(jnp.concatenate([hi, lo], axis=1), u_ref[...], preferred_element_type=F32)
            a_c = jnp.exp2(z[:, sl] - g - _rep(carry, sub // LANES))
            if strict is not None:
                a_c = jnp.where(strict[:, sl], a_c, 0.0)
            a_scr[:, sl] = a_c.astype(BF16)
            carry = carry + jnp.sum(sp_c, axis=-1, keepdims=True)
        carry_scr[...] = carry

    carry_scr[...] = jnp.zeros_like(carry_scr)
    acc_scr[...] = jnp.zeros_like(acc_scr)
    row = lax.broadcasted_iota(jnp.int32, (blk, blk), 0)
    col = lax.broadcasted_iota(jnp.int32, (blk, blk), 1)
    weights(logits(q0), col < row)
    z_scr[...] = logits(block_start(1))

    def body(n, c):
        acc_scr[...] += weighted_values(block_start(n - 1))
        z = z_scr[...]
        z_scr[...] = logits(block_start(n + 1))
        weights(z, None)
        return c

    lax.fori_loop(1, qi + 1, body, 0)
    o_ref[...] = (acc_scr[...] + weighted_values(block_start(qi))).astype(o_ref.dtype)


def sb_attention(proj, *, batch, seq, blk, sub):
    m = proj.shape[0]
    h, dh = SB_HEADS, HEAD_DIM
    assert seq % blk == 0 and blk % sub == 0
    nblk = seq // blk
    tri = np.tril(np.ones((sub, sub), np.float32))
    u = jnp.asarray(np.concatenate([tri, tri], axis=0), BF16)
    kern = functools.partial(_sb_attn_kernel, blk=blk, sub=sub)
    return pl.pallas_call(
        kern, grid=(batch, h, nblk),
        in_specs=[
            pl.BlockSpec((blk, dh), lambda b, hh, i: (b * nblk + i, hh)),
            pl.BlockSpec((dh, seq), lambda b, hh, i: (hh, b)),
            pl.BlockSpec((seq, dh), lambda b, hh, i: (b, 2 * h + hh)),
            pl.BlockSpec((2 * sub, sub), lambda b, hh, i: (0, 0)),
        ],
        out_specs=pl.BlockSpec((blk, dh), lambda b, hh, i: (b * nblk + i, hh)),
        out_shape=jax.ShapeDtypeStruct((m, h * dh), BF16),
        scratch_shapes=[pltpu.VMEM((blk, LANES), F32), pltpu.VMEM((blk, dh), F32),
                        pltpu.VMEM((blk, blk), F32), pltpu.VMEM((blk, blk), BF16)],
        compiler_params=_params(("parallel", "parallel", "arbitrary")),
        name="sb_attention",
    )(proj, _keys_transposed(proj, h * dh), proj, u)


LRU_TN = 3 * LANES
LRU_KW = 7 * LANES
LRU_HALO = 8


def _lru_windows(width, bd):
    starts = []
    for j in range(width // LRU_TN):
        c0, c1 = j * LRU_TN, (j + 1) * LRU_TN - 1
        r0, r1 = (c0 // bd) * bd, (c1 // bd + 1) * bd
        k0 = min((r0 // LANES) * LANES, width - LRU_KW)
        assert k0 <= r0 and r1 <= k0 + LRU_KW
        starts.append(k0)
    return starts


def _pack_lru_gates(w_r, w_i, width):
    nb, bd, _ = w_r.shape
    starts = _lru_windows(width, bd)
    eye = jnp.eye(nb, dtype=w_r.dtype)
    dense_r = jnp.einsum('nde,nm->ndme', w_r, eye).reshape(width, width)
    dense_i = jnp.einsum('nde,nm->ndme', w_i, eye).reshape(width, width)
    tiles = []
    for j, k0 in enumerate(starts):
        cs = slice(j * LRU_TN, (j + 1) * LRU_TN)
        tiles.append(jnp.concatenate([dense_r[k0:k0 + LRU_KW, cs], dense_i[k0:k0 + LRU_KW, cs]], axis=1))
    return jnp.stack(tiles).astype(BF16), starts


def _lru_kernel(gate_ref, u_ref, cw_ref, cb_ref, wri_ref, br_ref, bi_ref, lam_ref, y_ref,
                ubuf, a_scr, b_scr, h_scr, *, starts, t):
    sblk = pl.program_id(1)
    width = u_ref.shape[1]

    @pl.when(sblk == 0)
    def _():
        ubuf[0:LRU_HALO, :] = jnp.zeros((LRU_HALO, width), F32)
        h_scr[...] = jnp.zeros_like(h_scr)

    ubuf[LRU_HALO:LRU_HALO + t, :] = u_ref[...]
    nconv = cw_ref.shape[0]
    uc = cb_ref[...] + cw_ref[nconv - 1:nconv, :] * u_ref[...]
    for j in range(nconv - 1):
        off = LRU_HALO - (nconv - 1) + j
        uc = uc + cw_ref[j:j + 1, :] * ubuf[off:off + t, :]
    ubuf[0:LRU_HALO, :] = u_ref[t - LRU_HALO:t, :]

    ucb = uc.astype(BF16)
    log_sig_lam = _log_sigmoid(lam_ref[...])
    row_in_group = lax.broadcasted_iota(jnp.int32, (t, LRU_TN), 0) % SUBLANES
    for j, k0 in enumerate(starts):
        cs = slice(j * LRU_TN, (j + 1) * LRU_TN)
        pre = jnp.dot(ucb[:, k0:k0 + LRU_KW], wri_ref[j], preferred_element_type=F32)
        r = jax.nn.sigmoid(pre[:, :LRU_TN] + br_ref[:, cs])
        ig = jax.nn.sigmoid(pre[:, LRU_TN:] + bi_ref[:, cs])
        log_a = LRU_C * r * log_sig_lam[:, cs]
        a = jnp.exp(log_a)
        bterm = jnp.sqrt(1.0 - a * a) * (ig * uc[:, cs])
        d = 1
        while d < SUBLANES:
            take = row_in_group >= d
            b_sh = pltpu.roll(bterm, d, 0)
            a_sh = pltpu.roll(a, d, 0)
            bterm = jnp.where(take, a * b_sh + bterm, bterm)
            a = jnp.where(take, a * a_sh, a)
            d *= 2
        a_scr[:, cs] = a
        b_scr[:, cs] = bterm

    h_in = h_scr[...]
    for r in range(t // SUBLANES):
        rows = slice(r * SUBLANES, (r + 1) * SUBLANES)
        h = b_scr[rows, :] + a_scr[rows, :] * h_in
        b_scr[rows, :] = h
        h_in = h[SUBLANES - 1:SUBLANES, :]
    h_scr[...] = h_in
    y_ref[...] = (b_scr[...] * jax.nn.gelu(gate_ref[...], approximate=True)).astype(y_ref.dtype)


def lru_core(proj, conv_w, conv_b, wri, starts, b_r, b_i, lam, *, batch, seq, t):
    m = proj.shape[0]
    width = proj.shape[1] // 2
    nt = seq // t
    assert seq % t == 0 and t % SUBLANES == 0
    vec = lambda: pl.BlockSpec((1, width), lambda b, s: (0, 0))
    kern = functools.partial(_lru_kernel, starts=tuple(starts), t=t)
    return pl.pallas_call(
        kern, grid=(batch, nt),
        in_specs=[pl.BlockSpec((t, width), lambda b, s: (b * nt + s, 0)),
                  pl.BlockSpec((t, width), lambda b, s: (b * nt + s, 1)),
                  pl.BlockSpec(conv_w.shape, lambda b, s: (0, 0)),
                  vec(),
                  pl.BlockSpec(wri.shape, lambda b, s: (0, 0, 0)),
                  vec(), vec(), vec()],
        out_specs=pl.BlockSpec((t, width), lambda b, s: (b * nt + s, 0)),
        out_shape=jax.ShapeDtypeStruct((m, width), BF16),
        scratch_shapes=[pltpu.VMEM((t + LRU_HALO, width), F32),
                        pltpu.VMEM((t, width), F32),
                        pltpu.VMEM((t, width), F32),
                        pltpu.VMEM((1, width), F32)],
        compiler_params=_params(("parallel", "arbitrary")),
        name="lru_core",
    )(proj, proj, conv_w, conv_b.reshape(1, width), wri, b_r.reshape(1, width),
      b_i.reshape(1, width), lam.reshape(1, width))


def _col_to_row(col, eye):
    return jnp.sum(jnp.where(eye, col, 0.0), axis=0, keepdims=True)


def _mlstm_kernel(q_ref, k_ref, v_ref, o_ref, g_ref, bif_ref, hg_ref, y_ref,
                  c_scr, n_scr, m_scr, *, nheads, kscale):
    head = pl.program_id(1)
    chunk = pl.program_id(2)
    length = q_ref.shape[0]

    @pl.when(chunk == 0)
    def _():
        c_scr[...] = jnp.zeros_like(c_scr)
        n_scr[...] = jnp.zeros_like(n_scr)
        m_scr[...] = jnp.full_like(m_scr, MLSTM_M_INIT)

    row = lax.broadcasted_iota(jnp.int32, (length, length), 0)
    col = lax.broadcasted_iota(jnp.int32, (length, length), 1)
    eye = row == col
    causal = col <= row

    g = g_ref[...] + bif_ref[...]
    lane = lax.broadcasted_iota(jnp.int32, g.shape, 1)
    i_col = jnp.sum(jnp.where(lane == head, g, 0.0), axis=-1, keepdims=True)
    f_col = jnp.sum(jnp.where(lane == nheads + head, g, 0.0), axis=-1, keepdims=True)
    lf_col = _log_sigmoid(f_col)
    lf_row = _col_to_row(lf_col, eye)
    i_row = _col_to_row(i_col, eye)
    b_col = jnp.sum(jnp.where(causal, lf_row, 0.0), axis=-1, keepdims=True)
    b_row = _col_to_row(b_col, eye)
    gsum = jnp.sum(lf_row, axis=-1, keepdims=True)

    m_prev = m_scr[...]
    dm = jnp.where(causal, b_col - b_row + i_row, NEG)
    inter = b_col + m_prev
    m_t = jnp.maximum(inter, jnp.max(dm, axis=-1, keepdims=True))
    w_intra = jnp.exp(dm - m_t)
    w_inter = jnp.exp(inter - m_t)

    q = q_ref[...]
    ks = k_ref[...] * kscale
    v = v_ref[...]
    s = lax.dot_general(q, ks, (((1,), (1,)), ((), ())), preferred_element_type=F32) * w_intra
    num = (w_inter * jnp.dot(q, c_scr[...].astype(BF16), preferred_element_type=F32)
           + jnp.dot(s.astype(BF16), v, preferred_element_type=F32))
    qn = jnp.sum(q.astype(F32) * n_scr[...], axis=-1, keepdims=True)
    den = w_inter * qn + jnp.sum(s, axis=-1, keepdims=True)
    hval = num / jnp.maximum(jnp.abs(den), jnp.exp(-m_t))

    hn = hval * _rms_scale(hval) * hg_ref[...]
    y_ref[...] = (jax.nn.sigmoid(o_ref[...].astype(F32)) * hn).astype(y_ref.dtype)

    wk_col = gsum - b_col + i_col
    m_new = jnp.maximum(gsum + m_prev, jnp.max(wk_col, axis=0, keepdims=True))
    decay = jnp.exp(gsum + m_prev - m_new)
    kw = jnp.exp(wk_col - m_new) * ks.astype(F32)
    c_scr[...] = decay * c_scr[...] + lax.dot_general(
        kw.astype(BF16), v, (((0,), (0,)), ((), ())), preferred_element_type=F32)
    n_scr[...] = decay * n_scr[...] + jnp.sum(kw, axis=0, keepdims=True)
    m_scr[...] = m_new


def mlstm_core(proj, gates, b_if, head_g, *, batch, seq, chunk):
    m = proj.shape[0]
    d = head_g.shape[0]
    nh = MLSTM_HEADS
    dk, dv = d // 2 // nh, d // nh
    nc = seq // chunk
    ng = gates.shape[1]
    bif = jnp.zeros((1, ng), F32).at[0, :2 * nh].set(b_if.reshape(-1))
    kern = functools.partial(_mlstm_kernel, nheads=nh, kscale=dk ** -0.5)
    kblk0, vblk0, oblk0 = (nh * dk) // dk, (2 * nh * dk) // dv, (2 * nh * dk + d) // dv
    return pl.pallas_call(
        kern, grid=(batch, nh, nc),
        in_specs=[pl.BlockSpec((chunk, dk), lambda b, h, c: (b * nc + c, h)),
                  pl.BlockSpec((chunk, dk), lambda b, h, c: (b * nc + c, kblk0 + h)),
                  pl.BlockSpec((chunk, dv), lambda b, h, c: (b * nc + c, vblk0 + h)),
                  pl.BlockSpec((chunk, dv), lambda b, h, c: (b * nc + c, oblk0 + h)),
                  pl.BlockSpec((chunk, ng), lambda b, h, c: (b * nc + c, 0)),
                  pl.BlockSpec((1, ng), lambda b, h, c: (0, 0)),
                  pl.BlockSpec((1, dv), lambda b, h, c: (0, h))],
        out_specs=pl.BlockSpec((chunk, dv), lambda b, h, c: (b * nc + c, h)),
        out_shape=jax.ShapeDtypeStruct((m, d), BF16),
        scratch_shapes=[pltpu.VMEM((dk, dv), F32), pltpu.VMEM((1, dk), F32), pltpu.VMEM((1, 1), F32)],
        compiler_params=_params(("parallel", "parallel", "arbitrary")),
        name="mlstm_core",
    )(proj, proj, proj, proj, gates, bif, head_g.reshape(1, d))


def _pad_cols(w, n):
    return jnp.pad(w, ((0, 0), (0, n - w.shape[1])))


def kernel(x, norm_g, mlp_w1, mlp_w2, fox_w_in, fox_b_f, fox_w_out, lru_w_in, lru_conv_w, lru_conv_b,
           lru_w_r, lru_b_r, lru_w_i, lru_b_i, lru_lambda, lru_w_out, sb_w_in, sb_w_out,
           mlstm_w_in, mlstm_b_if, mlstm_head_g, mlstm_w_out):
    batch, seq, d = x.shape
    depth = norm_g.shape[0]
    n_mixers = 4
    xf = x.reshape(batch * seq, d)
    tm_proj = 1024
    tm_out = 512
    attn_blk = min(512, seq)
    q_scale = HEAD_DIM ** -0.5 * LOG2E

    for i in range(depth):
        mixer, j = i % n_mixers, i // n_mixers
        if mixer == 0:
            w = fox_w_in[j]
            proj, fgate = norm_proj(xf, norm_g[i, 0], w[:, :3 * d].astype(BF16),
                                    _pad_cols(w[:, 3 * d:], LANES).astype(BF16),
                                    out_dtype=BF16, tm=tm_proj, tn=1024, q_scale=q_scale, q_cols=d)
            f_t = fgate[:, :FOX_HEADS].reshape(batch, seq, FOX_HEADS).transpose(0, 2, 1)
            cum = fox_cum(f_t, fox_b_f[j]).reshape(batch, FOX_HEADS, 1, seq)
            mixed = fox_attention(proj, cum, batch=batch, seq=seq, blk=attn_blk)
            w_out = fox_w_out[j]
        elif mixer == 1:
            width = lru_w_in.shape[2] // 2
            proj = norm_proj(xf, norm_g[i, 0], lru_w_in[j].astype(BF16), out_dtype=F32,
                             tm=tm_proj, tn=width // 3)
            wri, starts = _pack_lru_gates(lru_w_r[j], lru_w_i[j], width)
            mixed = lru_core(proj, lru_conv_w[j], lru_conv_b[j], wri, starts, lru_b_r[j], lru_b_i[j],
                             lru_lambda[j], batch=batch, seq=seq, t=min(256, seq))
            w_out = lru_w_out[j]
        elif mixer == 2:
            proj = norm_proj(xf, norm_g[i, 0], sb_w_in[j].astype(BF16), out_dtype=BF16,
                             tm=tm_proj, tn=1024, q_scale=q_scale, q_cols=d)
            mixed = sb_attention(proj, batch=batch, seq=seq, blk=attn_blk, sub=256)
            w_out = sb_w_out[j]
        else:
            w = mlstm_w_in[j]
            proj, gates = norm_proj(xf, norm_g[i, 0], w[:, :3 * d].astype(BF16),
                                    _pad_cols(w[:, 3 * d:], LANES).astype(BF16),
                                    out_dtype=BF16, tm=tm_proj, tn=1024)
            mixed = mlstm_core(proj, gates, mlstm_b_if[j], mlstm_head_g[j],
                               batch=batch, seq=seq, chunk=min(256, seq))
            w_out = mlstm_w_out[j]
        xf = proj_norm_res(mixed, w_out.astype(BF16), norm_g[i, 1], xf, tm=tm_out)
        xf = mlp_block(xf, norm_g[i, 2], mlp_w1[i].astype(BF16), mlp_w2[i].astype(BF16),
                       norm_g[i, 3], tm=512, tf=1024)
    return xf.reshape(batch, seq, d)
```

```python
import functools
import math

import numpy as np
import jax
import jax.numpy as jnp
from jax import lax
from jax.experimental import pallas as pl
from jax.experimental.pallas import tpu as pltpu

F32 = jnp.float32
BF16 = jnp.bfloat16

EPS = 1e-6
LANES = 128
SUBLANES = 8
VMEM_LIMIT = 56 * 1024 * 1024
NEG = -1e30
LOG2E = math.log2(math.e)

FOX_HEADS = 16
SB_HEADS = 16
HEAD_DIM = 128
LRU_BLOCKS = 16
LRU_C = 8.0
MLSTM_HEADS = 4
MLSTM_M_INIT = -1e30


def _params(sem, vmem=VMEM_LIMIT, flags=None):
    return pltpu.CompilerParams(dimension_semantics=sem, vmem_limit_bytes=vmem, flags=flags)


def _log_sigmoid(x):
    return jnp.minimum(x, 0.0) - jnp.log1p(jnp.exp(-jnp.abs(x)))


def _rms_scale(x):
    return lax.rsqrt(jnp.mean(x * x, axis=-1, keepdims=True) + EPS)


def _project(h_scr, w_ref, o_ref, q_scale, q_tiles):
    acc = jnp.dot(h_scr[...], w_ref[...], preferred_element_type=F32)
    if q_tiles:
        acc = acc * jnp.where(pl.program_id(1) < q_tiles, q_scale, 1.0)
    o_ref[...] = acc.astype(o_ref.dtype)


def _norm_proj_kernel(x_ref, g_ref, w_ref, o_ref, h_scr, *, q_scale, q_tiles):
    @pl.when(pl.program_id(1) == 0)
    def _():
        x = x_ref[...]
        h_scr[...] = (x * _rms_scale(x) * g_ref[...]).astype(BF16)

    _project(h_scr, w_ref, o_ref, q_scale, q_tiles)


def _norm_proj_gate_kernel(x_ref, g_ref, w_ref, wg_ref, o_ref, og_ref, h_scr, *, q_scale, q_tiles):
    @pl.when(pl.program_id(1) == 0)
    def _():
        x = x_ref[...]
        h = (x * _rms_scale(x) * g_ref[...]).astype(BF16)
        h_scr[...] = h
        og_ref[...] = jnp.dot(h, wg_ref[...], preferred_element_type=F32)

    _project(h_scr, w_ref, o_ref, q_scale, q_tiles)


def norm_proj(x, g, w, wg=None, *, out_dtype, tm, tn, q_scale=1.0, q_cols=0):
    m, d = x.shape
    n = w.shape[1]
    assert m % tm == 0 and n % tn == 0 and q_cols % tn == 0
    grid = (m // tm, n // tn)
    x_spec = pl.BlockSpec((tm, d), lambda i, j: (i, 0))
    g_spec = pl.BlockSpec((1, d), lambda i, j: (0, 0))
    w_spec = pl.BlockSpec((d, tn), lambda i, j: (0, j))
    o_spec = pl.BlockSpec((tm, tn), lambda i, j: (i, j))
    scratch = [pltpu.VMEM((tm, d), BF16)]
    statics = dict(q_scale=q_scale, q_tiles=q_cols // tn)
    if wg is None:
        return pl.pallas_call(
            functools.partial(_norm_proj_kernel, **statics), grid=grid,
            in_specs=[x_spec, g_spec, w_spec], out_specs=o_spec,
            out_shape=jax.ShapeDtypeStruct((m, n), out_dtype),
            scratch_shapes=scratch,
            compiler_params=_params(("parallel", "arbitrary")),
            name="norm_proj",
        )(x, g.reshape(1, d), w)
    ng = wg.shape[1]
    return pl.pallas_call(
        functools.partial(_norm_proj_gate_kernel, **statics), grid=grid,
        in_specs=[x_spec, g_spec, w_spec, pl.BlockSpec((d, ng), lambda i, j: (0, 0))],
        out_specs=[o_spec, pl.BlockSpec((tm, ng), lambda i, j: (i, 0))],
        out_shape=[jax.ShapeDtypeStruct((m, n), out_dtype), jax.ShapeDtypeStruct((m, ng), F32)],
        scratch_shapes=scratch,
        compiler_params=_params(("parallel", "arbitrary")),
        name="norm_proj_gate",
    )(x, g.reshape(1, d), w, wg)


def _proj_norm_res_kernel(a_ref, w_ref, g_ref, x_ref, o_ref):
    y = jnp.dot(a_ref[...], w_ref[...], preferred_element_type=F32)
    o_ref[...] = x_ref[...] + y * _rms_scale(y) * g_ref[...]


def proj_norm_res(a, w, g, x, *, tm):
    m, k = a.shape
    d = w.shape[1]
    assert m % tm == 0
    return pl.pallas_call(
        _proj_norm_res_kernel, grid=(m // tm,),
        in_specs=[pl.BlockSpec((tm, k), lambda i: (i, 0)),
                  pl.BlockSpec((k, d), lambda i: (0, 0)),
                  pl.BlockSpec((1, d), lambda i: (0, 0)),
                  pl.BlockSpec((tm, d), lambda i: (i, 0))],
        out_specs=pl.BlockSpec((tm, d), lambda i: (i, 0)),
        out_shape=jax.ShapeDtypeStruct((m, d), F32),
        compiler_params=_params(("parallel",)),
        name="proj_norm_res",
    )(a, w, g.reshape(1, d), x)


def _mlp_kernel(x_ref, g2_ref, w1_ref, w2_ref, g3_ref, o_ref, h_scr, acc_scr):
    f = pl.program_id(1)

    @pl.when(f == 0)
    def _():
        x = x_ref[...]
        h_scr[...] = (x * _rms_scale(x) * g2_ref[...]).astype(BF16)
        acc_scr[...] = jnp.zeros_like(acc_scr)

    a = jnp.dot(h_scr[...], w1_ref[...], preferred_element_type=F32)
    a = jnp.square(jnp.maximum(a, 0.0)).astype(BF16)
    acc_scr[...] += jnp.dot(a, w2_ref[...], preferred_element_type=F32)

    @pl.when(f == pl.num_programs(1) - 1)
    def _():
        y = acc_scr[...]
        o_ref[...] = x_ref[...] + y * _rms_scale(y) * g3_ref[...]


def mlp_block(x, g2, w1, w2, g3, *, tm, tf):
    m, d = x.shape
    ff = w1.shape[1]
    assert m % tm == 0 and ff % tf == 0
    return pl.pallas_call(
        _mlp_kernel, grid=(m // tm, ff // tf),
        in_specs=[pl.BlockSpec((tm, d), lambda i, f: (i, 0)),
                  pl.BlockSpec((1, d), lambda i, f: (0, 0)),
                  pl.BlockSpec((d, tf), lambda i, f: (0, f)),
                  pl.BlockSpec((tf, d), lambda i, f: (f, 0)),
                  pl.BlockSpec((1, d), lambda i, f: (0, 0))],
        out_specs=pl.BlockSpec((tm, d), lambda i, f: (i, 0)),
        out_shape=jax.ShapeDtypeStruct((m, d), F32),
        scratch_shapes=[pltpu.VMEM((tm, d), BF16), pltpu.VMEM((tm, d), F32)],
        compiler_params=_params(("parallel", "arbitrary")),
        name="mlp_block",
    )(x, g2.reshape(1, d), w1, w2, g3.reshape(1, d))


def _fox_cum_kernel(f_ref, b_ref, c_ref, carry_scr):
    nblk = f_ref.shape[1] // LANES
    lane = lax.broadcasted_iota(jnp.int32, (f_ref.shape[0], LANES), 1)
    carry_scr[...] = jnp.zeros_like(carry_scr)

    def body(j, _):
        off = pl.multiple_of(j * LANES, LANES)
        v = _log_sigmoid(f_ref[:, pl.ds(off, LANES)] + b_ref[...])
        d = 1
        while d < LANES:
            v = v + jnp.where(lane >= d, pltpu.roll(v, d, 1), 0.0)
            d *= 2
        v = v + carry_scr[...]
        c_ref[:, pl.ds(off, LANES)] = v * LOG2E
        carry_scr[...] = v[:, LANES - 1:LANES]
        return 0

    lax.fori_loop(0, nblk, body, 0)


def fox_cum(f_t, b_f):
    b, h, s = f_t.shape
    return pl.pallas_call(
        _fox_cum_kernel, grid=(b,),
        in_specs=[pl.BlockSpec((None, h, s), lambda i: (i, 0, 0)),
                  pl.BlockSpec((h, 1), lambda i: (0, 0))],
        out_specs=pl.BlockSpec((None, h, s), lambda i: (i, 0, 0)),
        out_shape=jax.ShapeDtypeStruct((b, h, s), F32),
        scratch_shapes=[pltpu.VMEM((h, 1), F32)],
        compiler_params=_params(("parallel",)),
        name="fox_cum",
    )(f_t, b_f.reshape(h, 1))


ATTN_FLAGS = None


def _keys_transposed(proj, d):
    return proj[:, d:2 * d].T


def _rep(x, n):
    return jnp.concatenate([x] * n, axis=1) if n > 1 else x


FOX_HEADS_PER_STEP = 4


def _fox_attn_kernel(q_ref, kt_ref, v_ref, c_ref, o_ref, m_scr, acc_scr, *, blk, nh):
    qi = pl.program_id(2)
    q0 = pl.multiple_of(qi * blk, blk)
    dh = HEAD_DIM
    ones = jnp.ones((blk, LANES), BF16)

    def block(h, kstart, causal):
        hs = slice(h * dh, (h + 1) * dh)
        s = jnp.dot(q_ref[:, hs], kt_ref[hs, pl.ds(kstart, blk)], preferred_element_type=F32)
        c0 = c_ref[h, :, pl.ds(q0, LANES)][:, 0:1]
        s = s + (c0 - c_ref[h, :, pl.ds(kstart, blk)])
        if causal is not None:
            s = jnp.where(causal, s, NEG)
        m_prev = m_scr[h]
        m_new = jnp.maximum(m_prev, jnp.max(s, axis=-1, keepdims=True))
        p = jnp.exp2(s - _rep(m_new, blk // LANES)).astype(BF16)
        alpha = jnp.exp2(m_prev - m_new)
        v = jnp.concatenate([v_ref[pl.ds(kstart, blk), hs], ones], axis=1)
        acc_scr[h] = _rep(alpha, 2) * acc_scr[h] + jnp.dot(p, v, preferred_element_type=F32)
        m_scr[h] = m_new

    m_scr[...] = jnp.full_like(m_scr, NEG)
    acc_scr[...] = jnp.zeros_like(acc_scr)

    def body(j, carry):
        kstart = pl.multiple_of(j * blk, blk)
        for h in range(nh):
            block(h, kstart, None)
        return carry

    lax.fori_loop(0, qi, body, 0)
    row = lax.broadcasted_iota(jnp.int32, (blk, blk), 0)
    col = lax.broadcasted_iota(jnp.int32, (blk, blk), 1)
    for h in range(nh):
        block(h, q0, col <= row)
        total = acc_scr[h]
        o_ref[:, h * dh:(h + 1) * dh] = (total[:, :dh] / total[:, dh:]).astype(o_ref.dtype)


def fox_attention(proj, cum, *, batch, seq, blk):
    m = proj.shape[0]
    h, dh, nh = FOX_HEADS, HEAD_DIM, FOX_HEADS_PER_STEP
    assert dh == LANES and seq % blk == 0 and h % nh == 0
    nblk = seq // blk
    w = nh * dh
    kern = functools.partial(_fox_attn_kernel, blk=blk, nh=nh)
    return pl.pallas_call(
        kern, grid=(batch, h // nh, nblk),
        in_specs=[
            pl.BlockSpec((blk, w), lambda b, hh, i: (b * nblk + i, hh)),
            pl.BlockSpec((w, seq), lambda b, hh, i: (hh, b)),
            pl.BlockSpec((seq, w), lambda b, hh, i: (b, 2 * (h // nh) + hh)),
            pl.BlockSpec((None, nh, 1, seq), lambda b, hh, i: (b, hh, 0, 0)),
        ],
        out_specs=pl.BlockSpec((blk, w), lambda b, hh, i: (b * nblk + i, hh)),
        out_shape=jax.ShapeDtypeStruct((m, h * dh), BF16),
        scratch_shapes=[pltpu.VMEM((nh, blk, LANES), F32), pltpu.VMEM((nh, blk, dh + LANES), F32)],
        compiler_params=_params(("parallel", "parallel", "arbitrary"), flags=ATTN_FLAGS),
        name="fox_attention",
    )(proj, _keys_transposed(proj, h * dh), proj, cum)


SB_LOGIT_CAP = 126.0


SB_HEADS_PER_STEP = 4


def _sb_attn_kernel(q_ref, kt_ref, v_ref, u_ref, o_ref, carry_scr, acc_scr, *, blk, sub, nh):
    qi = pl.program_id(2)
    q0 = pl.multiple_of(qi * blk, blk)
    nsub = blk // sub
    dh = HEAD_DIM

    def block(h, kstart, strict):
        hs = slice(h * dh, (h + 1) * dh)
        z = jnp.dot(q_ref[:, hs], kt_ref[hs, pl.ds(kstart, blk)], preferred_element_type=F32)
        z = jnp.minimum(z, SB_LOGIT_CAP)
        sp = jnp.log(1.0 + jnp.exp2(z)) * LOG2E
        if strict is not None:
            sp = jnp.where(strict, sp, 0.0)
        carry = carry_scr[h]
        parts = [None] * nsub
        for c in range(nsub - 1, -1, -1):
            sl = slice(c * sub, (c + 1) * sub)
            sp_c = sp[:, sl]
            g = jnp.dot(sp_c.astype(BF16), u_ref[...], preferred_element_type=F32)
            a_c = jnp.exp2(z[:, sl] - g - _rep(carry, sub // LANES))
            if strict is not None:
                a_c = jnp.where(strict[:, sl], a_c, 0.0)
            parts[c] = a_c.astype(BF16)
            carry = carry + jnp.sum(sp_c, axis=-1, keepdims=True)
        carry_scr[h] = carry
        acc_scr[:, hs] += jnp.dot(jnp.concatenate(parts, axis=1), v_ref[pl.ds(kstart, blk), hs],
                                  preferred_element_type=F32)

    carry_scr[...] = jnp.zeros_like(carry_scr)
    acc_scr[...] = jnp.zeros_like(acc_scr)
    row = lax.broadcasted_iota(jnp.int32, (blk, blk), 0)
    col = lax.broadcasted_iota(jnp.int32, (blk, blk), 1)
    for h in range(nh):
        block(h, q0, col < row)

    def body(n, c):
        kstart = pl.multiple_of((qi - n) * blk, blk)
        for h in range(nh):
            block(h, kstart, None)
        return c

    lax.fori_loop(1, qi + 1, body, 0)
    o_ref[...] = acc_scr[...].astype(o_ref.dtype)


def sb_attention(proj, *, batch, seq, blk, sub):
    m = proj.shape[0]
    h, dh, nh = SB_HEADS, HEAD_DIM, SB_HEADS_PER_STEP
    assert seq % blk == 0 and blk % sub == 0 and h % nh == 0
    nblk = seq // blk
    w = nh * dh
    u = jnp.asarray(np.tril(np.ones((sub, sub), np.float32)), BF16)
    kern = functools.partial(_sb_attn_kernel, blk=blk, sub=sub, nh=nh)
    return pl.pallas_call(
        kern, grid=(batch, h // nh, nblk),
        in_specs=[
            pl.BlockSpec((blk, w), lambda b, hh, i: (b * nblk + i, hh)),
            pl.BlockSpec((w, seq), lambda b, hh, i: (hh, b)),
            pl.BlockSpec((seq, w), lambda b, hh, i: (b, 2 * (h // nh) + hh)),
            pl.BlockSpec((sub, sub), lambda b, hh, i: (0, 0)),
        ],
        out_specs=pl.BlockSpec((blk, w), lambda b, hh, i: (b * nblk + i, hh)),
        out_shape=jax.ShapeDtypeStruct((m, h * dh), BF16),
        scratch_shapes=[pltpu.VMEM((nh, blk, LANES), F32), pltpu.VMEM((blk, w), F32)],
        compiler_params=_params(("parallel", "parallel", "arbitrary")),
        name="sb_attention",
    )(proj, _keys_transposed(proj, h * dh), proj, u)


LRU_TN = 3 * LANES
LRU_KW = 7 * LANES
LRU_HALO = 8


def _lru_windows(width, bd):
    starts = []
    for j in range(width // LRU_TN):
        c0, c1 = j * LRU_TN, (j + 1) * LRU_TN - 1
        r0, r1 = (c0 // bd) * bd, (c1 // bd + 1) * bd
        k0 = min((r0 // LANES) * LANES, width - LRU_KW)
        assert k0 <= r0 and r1 <= k0 + LRU_KW
        starts.append(k0)
    return starts


def _pack_lru_gates(w_r, w_i, width):
    nb, bd, _ = w_r.shape
    starts = _lru_windows(width, bd)
    eye = jnp.eye(nb, dtype=w_r.dtype)
    dense_r = jnp.einsum('nde,nm->ndme', w_r, eye).reshape(width, width)
    dense_i = jnp.einsum('nde,nm->ndme', w_i, eye).reshape(width, width)
    tiles = []
    for j, k0 in enumerate(starts):
        cs = slice(j * LRU_TN, (j + 1) * LRU_TN)
        tiles.append(jnp.concatenate([dense_r[k0:k0 + LRU_KW, cs], dense_i[k0:k0 + LRU_KW, cs]], axis=1))
    return jnp.stack(tiles).astype(BF16), starts


def _lru_kernel(gate_ref, u_ref, cw_ref, cb_ref, wri_ref, br_ref, bi_ref, lam_ref, y_ref,
                ubuf, a_scr, b_scr, h_scr, *, starts, t):
    sblk = pl.program_id(1)
    width = u_ref.shape[1]

    @pl.when(sblk == 0)
    def _():
        ubuf[0:LRU_HALO, :] = jnp.zeros((LRU_HALO, width), F32)
        h_scr[...] = jnp.zeros_like(h_scr)

    ubuf[LRU_HALO:LRU_HALO + t, :] = u_ref[...]
    nconv = cw_ref.shape[0]
    uc = cb_ref[...] + cw_ref[nconv - 1:nconv, :] * u_ref[...]
    for j in range(nconv - 1):
        off = LRU_HALO - (nconv - 1) + j
        uc = uc + cw_ref[j:j + 1, :] * ubuf[off:off + t, :]
    ubuf[0:LRU_HALO, :] = u_ref[t - LRU_HALO:t, :]

    ucb = uc.astype(BF16)
    log_sig_lam = _log_sigmoid(lam_ref[...])
    row_in_group = lax.broadcasted_iota(jnp.int32, (t, LRU_TN), 0) % SUBLANES
    for j, k0 in enumerate(starts):
        cs = slice(j * LRU_TN, (j + 1) * LRU_TN)
        pre = jnp.dot(ucb[:, k0:k0 + LRU_KW], wri_ref[j], preferred_element_type=F32)
        r = jax.nn.sigmoid(pre[:, :LRU_TN] + br_ref[:, cs])
        ig = jax.nn.sigmoid(pre[:, LRU_TN:] + bi_ref[:, cs])
        log_a = LRU_C * r * log_sig_lam[:, cs]
        a = jnp.exp(log_a)
        bterm = jnp.sqrt(1.0 - a * a) * (ig * uc[:, cs])
        d = 1
        while d < SUBLANES:
            take = row_in_group >= d
            b_sh = pltpu.roll(bterm, d, 0)
            a_sh = pltpu.roll(a, d, 0)
            bterm = jnp.where(take, a * b_sh + bterm, bterm)
            a = jnp.where(take, a * a_sh, a)
            d *= 2
        a_scr[:, cs] = a
        b_scr[:, cs] = bterm

    h_in = h_scr[...]
    for r in range(t // SUBLANES):
        rows = slice(r * SUBLANES, (r + 1) * SUBLANES)
        h = b_scr[rows, :] + a_scr[rows, :] * h_in
        b_scr[rows, :] = h
        h_in = h[SUBLANES - 1:SUBLANES, :]
    h_scr[...] = h_in
    y_ref[...] = (b_scr[...] * jax.nn.gelu(gate_ref[...], approximate=True)).astype(y_ref.dtype)


def lru_core(proj, conv_w, conv_b, wri, starts, b_r, b_i, lam, *, batch, seq, t):
    m = proj.shape[0]
    width = proj.shape[1] // 2
    nt = seq // t
    assert seq % t == 0 and t % SUBLANES == 0
    vec = lambda: pl.BlockSpec((1, width), lambda b, s: (0, 0))
    kern = functools.partial(_lru_kernel, starts=tuple(starts), t=t)
    return pl.pallas_call(
        kern, grid=(batch, nt),
        in_specs=[pl.BlockSpec((t, width), lambda b, s: (b * nt + s, 0)),
                  pl.BlockSpec((t, width), lambda b, s: (b * nt + s, 1)),
                  pl.BlockSpec(conv_w.shape, lambda b, s: (0, 0)),
                  vec(),
                  pl.BlockSpec(wri.shape, lambda b, s: (0, 0, 0)),
                  vec(), vec(), vec()],
        out_specs=pl.BlockSpec((t, width), lambda b, s: (b * nt + s, 0)),
        out_shape=jax.ShapeDtypeStruct((m, width), BF16),
        scratch_shapes=[pltpu.VMEM((t + LRU_HALO, width), F32),
                        pltpu.VMEM((t, width), F32),
                        pltpu.VMEM((t, width), F32),
                        pltpu.VMEM((1, width), F32)],
        compiler_params=_params(("parallel", "arbitrary")),
        name="lru_core",
    )(proj, proj, conv_w, conv_b.reshape(1, width), wri, b_r.reshape(1, width),
      b_i.reshape(1, width), lam.reshape(1, width))


def _col_to_row(col, eye):
    return jnp.sum(jnp.where(eye, col, 0.0), axis=0, keepdims=True)


def _mlstm_kernel(q_ref, k_ref, v_ref, o_ref, g_ref, bif_ref, hg_ref, y_ref,
                  c_scr, n_scr, m_scr, *, nheads, kscale):
    head = pl.program_id(1)
    chunk = pl.program_id(2)
    length = q_ref.shape[0]

    @pl.when(chunk == 0)
    def _():
        c_scr[...] = jnp.zeros_like(c_scr)
        n_scr[...] = jnp.zeros_like(n_scr)
        m_scr[...] = jnp.full_like(m_scr, MLSTM_M_INIT)

    row = lax.broadcasted_iota(jnp.int32, (length, length), 0)
    col = lax.broadcasted_iota(jnp.int32, (length, length), 1)
    eye = row == col
    causal = col <= row

    g = g_ref[...] + bif_ref[...]
    lane = lax.broadcasted_iota(jnp.int32, g.shape, 1)
    i_col = jnp.sum(jnp.where(lane == head, g, 0.0), axis=-1, keepdims=True)
    f_col = jnp.sum(jnp.where(lane == nheads + head, g, 0.0), axis=-1, keepdims=True)
    lf_col = _log_sigmoid(f_col)
    lf_row = _col_to_row(lf_col, eye)
    i_row = _col_to_row(i_col, eye)
    b_col = jnp.sum(jnp.where(causal, lf_row, 0.0), axis=-1, keepdims=True)
    b_row = _col_to_row(b_col, eye)
    gsum = jnp.sum(lf_row, axis=-1, keepdims=True)

    m_prev = m_scr[...]
    dm = jnp.where(causal, b_col - b_row + i_row, NEG)
    inter = b_col + m_prev
    m_t = jnp.maximum(inter, jnp.max(dm, axis=-1, keepdims=True))
    w_intra = jnp.exp(dm - m_t)
    w_inter = jnp.exp(inter - m_t)

    q = q_ref[...]
    ks = k_ref[...] * kscale
    v = v_ref[...]
    s = lax.dot_general(q, ks, (((1,), (1,)), ((), ())), preferred_element_type=F32) * w_intra
    num = (w_inter * jnp.dot(q, c_scr[...].astype(BF16), preferred_element_type=F32)
           + jnp.dot(s.astype(BF16), v, preferred_element_type=F32))
    qn = jnp.sum(q.astype(F32) * n_scr[...], axis=-1, keepdims=True)
    den = w_inter * qn + jnp.sum(s, axis=-1, keepdims=True)
    hval = num / jnp.maximum(jnp.abs(den), jnp.exp(-m_t))

    hn = hval * _rms_scale(hval) * hg_ref[...]
    y_ref[...] = (jax.nn.sigmoid(o_ref[...].astype(F32)) * hn).astype(y_ref.dtype)

    wk_col = gsum - b_col + i_col
    m_new = jnp.maximum(gsum + m_prev, jnp.max(wk_col, axis=0, keepdims=True))
    decay = jnp.exp(gsum + m_prev - m_new)
    kw = jnp.exp(wk_col - m_new) * ks.astype(F32)
    c_scr[...] = decay * c_scr[...] + lax.dot_general(
        kw.astype(BF16), v, (((0,), (0,)), ((), ())), preferred_element_type=F32)
    n_scr[...] = decay * n_scr[...] + jnp.sum(kw, axis=0, keepdims=True)
    m_scr[...] = m_new


def mlstm_core(proj, gates, b_if, head_g, *, batch, seq, chunk):
    m = proj.shape[0]
    d = head_g.shape[0]
    nh = MLSTM_HEADS
    dk, dv = d // 2 // nh, d // nh
    nc = seq // chunk
    ng = gates.shape[1]
    bif = jnp.zeros((1, ng), F32).at[0, :2 * nh].set(b_if.reshape(-1))
    kern = functools.partial(_mlstm_kernel, nheads=nh, kscale=dk ** -0.5)
    kblk0, vblk0, oblk0 = (nh * dk) // dk, (2 * nh * dk) // dv, (2 * nh * dk + d) // dv
    return pl.pallas_call(
        kern, grid=(batch, nh, nc),
        in_specs=[pl.BlockSpec((chunk, dk), lambda b, h, c: (b * nc + c, h)),
                  pl.BlockSpec((chunk, dk), lambda b, h, c: (b * nc + c, kblk0 + h)),
                  pl.BlockSpec((chunk, dv), lambda b, h, c: (b * nc + c, vblk0 + h)),
                  pl.BlockSpec((chunk, dv), lambda b, h, c: (b * nc + c, oblk0 + h)),
                  pl.BlockSpec((chunk, ng), lambda b, h, c: (b * nc + c, 0)),
                  pl.BlockSpec((1, ng), lambda b, h, c: (0, 0)),
                  pl.BlockSpec((1, dv), lambda b, h, c: (0, h))],
        out_specs=pl.BlockSpec((chunk, dv), lambda b, h, c: (b * nc + c, h)),
        out_shape=jax.ShapeDtypeStruct((m, d), BF16),
        scratch_shapes=[pltpu.VMEM((dk, dv), F32), pltpu.VMEM((1, dk), F32), pltpu.VMEM((1, 1), F32)],
        compiler_params=_params(("parallel", "parallel", "arbitrary")),
        name="mlstm_core",
    )(proj, proj, proj, proj, gates, bif, head_g.reshape(1, d))


def _pad_cols(w, n):
    return jnp.pad(w, ((0, 0), (0, n - w.shape[1])))


def kernel(x, norm_g, mlp_w1, mlp_w2, fox_w_in, fox_b_f, fox_w_out, lru_w_in, lru_conv_w, lru_conv_b,
           lru_w_r, lru_b_r, lru_w_i, lru_b_i, lru_lambda, lru_w_out, sb_w_in, sb_w_out,
           mlstm_w_in, mlstm_b_if, mlstm_head_g, mlstm_w_out):
    batch, seq, d = x.shape
    depth = norm_g.shape[0]
    n_mixers = 4
    xf = x.reshape(batch * seq, d)
    tm_proj = 1024
    tm_out = 512
    attn_blk = min(512, seq)
    q_scale = HEAD_DIM ** -0.5 * LOG2E

    for i in range(depth):
        mixer, j = i % n_mixers, i // n_mixers
        if mixer == 0:
            w = fox_w_in[j]
            proj, fgate = norm_proj(xf, norm_g[i, 0], w[:, :3 * d].astype(BF16),
                                    _pad_cols(w[:, 3 * d:], LANES).astype(BF16),
                                    out_dtype=BF16, tm=tm_proj, tn=1024, q_scale=q_scale, q_cols=d)
            f_t = fgate[:, :FOX_HEADS].reshape(batch, seq, FOX_HEADS).transpose(0, 2, 1)
            cum = fox_cum(f_t, fox_b_f[j]).reshape(batch, FOX_HEADS, 1, seq)
            mixed = fox_attention(proj, cum, batch=batch, seq=seq, blk=attn_blk)
            w_out = fox_w_out[j]
        elif mixer == 1:
            width = lru_w_in.shape[2] // 2
            proj = norm_proj(xf, norm_g[i, 0], lru_w_in[j].astype(BF16), out_dtype=F32,
                             tm=tm_proj, tn=width // 3)
            wri, starts = _pack_lru_gates(lru_w_r[j], lru_w_i[j], width)
            mixed = lru_core(proj, lru_conv_w[j], lru_conv_b[j], wri, starts, lru_b_r[j], lru_b_i[j],
                             lru_lambda[j], batch=batch, seq=seq, t=min(256, seq))
            w_out = lru_w_out[j]
        elif mixer == 2:
            proj = norm_proj(xf, norm_g[i, 0], sb_w_in[j].astype(BF16), out_dtype=BF16,
                             tm=tm_proj, tn=1024, q_scale=q_scale, q_cols=d)
            mixed = sb_attention(proj, batch=batch, seq=seq, blk=attn_blk, sub=256)
            w_out = sb_w_out[j]
        else:
            w = mlstm_w_in[j]
            proj, gates = norm_proj(xf, norm_g[i, 0], w[:, :3 * d].astype(BF16),
                                    _pad_cols(w[:, 3 * d:], LANES).astype(BF16),
                                    out_dtype=BF16, tm=tm_proj, tn=1024)
            mixed = mlstm_core(proj, gates, mlstm_b_if[j], mlstm_head_g[j],
                               batch=batch, seq=seq, chunk=min(256, seq))
            w_out = mlstm_w_out[j]
        xf = proj_norm_res(mixed, w_out.astype(BF16), norm_g[i, 1], xf, tm=tm_out)
        xf = mlp_block(xf, norm_g[i, 2], mlp_w1[i].astype(BF16), mlp_w2[i].astype(BF16),
                       norm_g[i, 3], tm=512, tf=1024)
    return xf.reshape(batch, seq, d)
```

```python
import functools
import math

import numpy as np
import jax
import jax.numpy as jnp
from jax import lax
from jax.experimental import pallas as pl
from jax.experimental.pallas import tpu as pltpu

F32 = jnp.float32
BF16 = jnp.bfloat16

EPS = 1e-6
LANES = 128
SUBLANES = 8
VMEM_LIMIT = 56 * 1024 * 1024
NEG = -1e30
LOG2E = math.log2(math.e)

FOX_HEADS = 16
SB_HEADS = 16
HEAD_DIM = 128
LRU_BLOCKS = 16
LRU_C = 8.0
MLSTM_HEADS = 4
MLSTM_M_INIT = -1e30


def _params(sem, vmem=VMEM_LIMIT, flags=None):
    return pltpu.CompilerParams(dimension_semantics=sem, vmem_limit_bytes=vmem, flags=flags)


def _log_sigmoid(x):
    return jnp.minimum(x, 0.0) - jnp.log1p(jnp.exp(-jnp.abs(x)))


def _rms_scale(x):
    return lax.rsqrt(jnp.mean(x * x, axis=-1, keepdims=True) + EPS)


def _project(h_scr, w_ref, o_ref, q_scale, q_tiles):
    acc = jnp.dot(h_scr[...], w_ref[...], preferred_element_type=F32)
    if q_tiles:
        acc = acc * jnp.where(pl.program_id(1) < q_tiles, q_scale, 1.0)
    o_ref[...] = acc.astype(o_ref.dtype)


def _norm_proj_kernel(x_ref, g_ref, w_ref, o_ref, h_scr, *, q_scale, q_tiles):
    @pl.when(pl.program_id(1) == 0)
    def _():
        x = x_ref[...]
        h_scr[...] = (x * _rms_scale(x) * g_ref[...]).astype(BF16)

    _project(h_scr, w_ref, o_ref, q_scale, q_tiles)


def _norm_proj_gate_kernel(x_ref, g_ref, w_ref, wg_ref, o_ref, og_ref, h_scr, *, q_scale, q_tiles):
    @pl.when(pl.program_id(1) == 0)
    def _():
        x = x_ref[...]
        h = (x * _rms_scale(x) * g_ref[...]).astype(BF16)
        h_scr[...] = h
        og_ref[...] = jnp.dot(h, wg_ref[...], preferred_element_type=F32)

    _project(h_scr, w_ref, o_ref, q_scale, q_tiles)


def norm_proj(x, g, w, wg=None, *, out_dtype, tm, tn, q_scale=1.0, q_cols=0):
    m, d = x.shape
    n = w.shape[1]
    assert m % tm == 0 and n % tn == 0 and q_cols % tn == 0
    grid = (m // tm, n // tn)
    x_spec = pl.BlockSpec((tm, d), lambda i, j: (i, 0))
    g_spec = pl.BlockSpec((1, d), lambda i, j: (0, 0))
    w_spec = pl.BlockSpec((d, tn), lambda i, j: (0, j))
    o_spec = pl.BlockSpec((tm, tn), lambda i, j: (i, j))
    scratch = [pltpu.VMEM((tm, d), BF16)]
    statics = dict(q_scale=q_scale, q_tiles=q_cols // tn)
    if wg is None:
        return pl.pallas_call(
            functools.partial(_norm_proj_kernel, **statics), grid=grid,
            in_specs=[x_spec, g_spec, w_spec], out_specs=o_spec,
            out_shape=jax.ShapeDtypeStruct((m, n), out_dtype),
            scratch_shapes=scratch,
            compiler_params=_params(("parallel", "arbitrary")),
            name="norm_proj",
        )(x, g.reshape(1, d), w)
    ng = wg.shape[1]
    return pl.pallas_call(
        functools.partial(_norm_proj_gate_kernel, **statics), grid=grid,
        in_specs=[x_spec, g_spec, w_spec, pl.BlockSpec((d, ng), lambda i, j: (0, 0))],
        out_specs=[o_spec, pl.BlockSpec((tm, ng), lambda i, j: (i, 0))],
        out_shape=[jax.ShapeDtypeStruct((m, n), out_dtype), jax.ShapeDtypeStruct((m, ng), F32)],
        scratch_shapes=scratch,
        compiler_params=_params(("parallel", "arbitrary")),
        name="norm_proj_gate",
    )(x, g.reshape(1, d), w, wg)


def _proj_norm_res_kernel(a_ref, w_ref, g_ref, x_ref, o_ref):
    y = jnp.dot(a_ref[...], w_ref[...], preferred_element_type=F32)
    o_ref[...] = x_ref[...] + y * _rms_scale(y) * g_ref[...]


def proj_norm_res(a, w, g, x, *, tm):
    m, k = a.shape
    d = w.shape[1]
    assert m % tm == 0
    return pl.pallas_call(
        _proj_norm_res_kernel, grid=(m // tm,),
        in_specs=[pl.BlockSpec((tm, k), lambda i: (i, 0)),
                  pl.BlockSpec((k, d), lambda i: (0, 0)),
                  pl.BlockSpec((1, d), lambda i: (0, 0)),
                  pl.BlockSpec((tm, d), lambda i: (i, 0))],
        out_specs=pl.BlockSpec((tm, d), lambda i: (i, 0)),
        out_shape=jax.ShapeDtypeStruct((m, d), F32),
        compiler_params=_params(("parallel",)),
        name="proj_norm_res",
    )(a, w, g.reshape(1, d), x)


def _mlp_kernel(x_ref, g2_ref, w1_ref, w2_ref, g3_ref, o_ref, h_scr, acc_scr):
    f = pl.program_id(1)

    @pl.when(f == 0)
    def _():
        x = x_ref[...]
        h_scr[...] = (x * _rms_scale(x) * g2_ref[...]).astype(BF16)
        acc_scr[...] = jnp.zeros_like(acc_scr)

    a = jnp.dot(h_scr[...], w1_ref[...], preferred_element_type=F32)
    a = jnp.square(jnp.maximum(a, 0.0)).astype(BF16)
    acc_scr[...] += jnp.dot(a, w2_ref[...], preferred_element_type=F32)

    @pl.when(f == pl.num_programs(1) - 1)
    def _():
        y = acc_scr[...]
        o_ref[...] = x_ref[...] + y * _rms_scale(y) * g3_ref[...]


def mlp_block(x, g2, w1, w2, g3, *, tm, tf):
    m, d = x.shape
    ff = w1.shape[1]
    assert m % tm == 0 and ff % tf == 0
    return pl.pallas_call(
        _mlp_kernel, grid=(m // tm, ff // tf),
        in_specs=[pl.BlockSpec((tm, d), lambda i, f: (i, 0)),
                  pl.BlockSpec((1, d), lambda i, f: (0, 0)),
                  pl.BlockSpec((d, tf), lambda i, f: (0, f)),
                  pl.BlockSpec((tf, d), lambda i, f: (f, 0)),
                  pl.BlockSpec((1, d), lambda i, f: (0, 0))],
        out_specs=pl.BlockSpec((tm, d), lambda i, f: (i, 0)),
        out_shape=jax.ShapeDtypeStruct((m, d), F32),
        scratch_shapes=[pltpu.VMEM((tm, d), BF16), pltpu.VMEM((tm, d), F32)],
        compiler_params=_params(("parallel", "arbitrary")),
        name="mlp_block",
    )(x, g2.reshape(1, d), w1, w2, g3.reshape(1, d))


def _fox_cum_kernel(f_ref, b_ref, c_ref, carry_scr):
    nblk = f_ref.shape[1] // LANES
    lane = lax.broadcasted_iota(jnp.int32, (f_ref.shape[0], LANES), 1)
    carry_scr[...] = jnp.zeros_like(carry_scr)

    def body(j, _):
        off = pl.multiple_of(j * LANES, LANES)
        v = _log_sigmoid(f_ref[:, pl.ds(off, LANES)] + b_ref[...])
        d = 1
        while d < LANES:
            v = v + jnp.where(lane >= d, pltpu.roll(v, d, 1), 0.0)
            d *= 2
        v = v + carry_scr[...]
        c_ref[:, pl.ds(off, LANES)] = v * LOG2E
        carry_scr[...] = v[:, LANES - 1:LANES]
        return 0

    lax.fori_loop(0, nblk, body, 0)


def fox_cum(f_t, b_f):
    b, h, s = f_t.shape
    return pl.pallas_call(
        _fox_cum_kernel, grid=(b,),
        in_specs=[pl.BlockSpec((None, h, s), lambda i: (i, 0, 0)),
                  pl.BlockSpec((h, 1), lambda i: (0, 0))],
        out_specs=pl.BlockSpec((None, h, s), lambda i: (i, 0, 0)),
        out_shape=jax.ShapeDtypeStruct((b, h, s), F32),
        scratch_shapes=[pltpu.VMEM((h, 1), F32)],
        compiler_params=_params(("parallel",)),
        name="fox_cum",
    )(f_t, b_f.reshape(h, 1))


ATTN_FLAGS = None


def _rep(x, n):
    return jnp.concatenate([x] * n, axis=1) if n > 1 else x


FOX_HEADS_PER_STEP = 4


def _fox_attn_kernel(q_ref, k_ref, v_ref, c_ref, o_ref, m_scr, acc_scr, *, blk, nh):
    qi = pl.program_id(2)
    q0 = pl.multiple_of(qi * blk, blk)
    dh = HEAD_DIM
    ones = jnp.ones((blk, LANES), BF16)

    def block(h, kstart, causal):
        hs = slice(h * dh, (h + 1) * dh)
        s = lax.dot_general(q_ref[:, hs], k_ref[pl.ds(kstart, blk), hs], (((1,), (1,)), ((), ())),
                            preferred_element_type=F32)
        c0 = c_ref[h, :, pl.ds(q0, LANES)][:, 0:1]
        s = s + (c0 - c_ref[h, :, pl.ds(kstart, blk)])
        if causal is not None:
            s = jnp.where(causal, s, NEG)
        m_prev = m_scr[h]
        m_new = jnp.maximum(m_prev, jnp.max(s, axis=-1, keepdims=True))
        p = jnp.exp2(s - _rep(m_new, blk // LANES)).astype(BF16)
        alpha = jnp.exp2(m_prev - m_new)
        v = jnp.concatenate([v_ref[pl.ds(kstart, blk), hs], ones], axis=1)
        acc_scr[h] = _rep(alpha, 2) * acc_scr[h] + jnp.dot(p, v, preferred_element_type=F32)
        m_scr[h] = m_new

    m_scr[...] = jnp.full_like(m_scr, NEG)
    acc_scr[...] = jnp.zeros_like(acc_scr)

    def body(j, carry):
        kstart = pl.multiple_of(j * blk, blk)
        for h in range(nh):
            block(h, kstart, None)
        return carry

    lax.fori_loop(0, qi, body, 0)
    row = lax.broadcasted_iota(jnp.int32, (blk, blk), 0)
    col = lax.broadcasted_iota(jnp.int32, (blk, blk), 1)
    for h in range(nh):
        block(h, q0, col <= row)
        total = acc_scr[h]
        o_ref[:, h * dh:(h + 1) * dh] = (total[:, :dh] / total[:, dh:]).astype(o_ref.dtype)


def fox_attention(proj, cum, *, batch, seq, blk):
    m = proj.shape[0]
    h, dh, nh = FOX_HEADS, HEAD_DIM, FOX_HEADS_PER_STEP
    assert dh == LANES and seq % blk == 0 and h % nh == 0
    nblk = seq // blk
    w = nh * dh
    kern = functools.partial(_fox_attn_kernel, blk=blk, nh=nh)
    return pl.pallas_call(
        kern, grid=(batch, h // nh, nblk),
        in_specs=[
            pl.BlockSpec((blk, w), lambda b, hh, i: (b * nblk + i, hh)),
            pl.BlockSpec((seq, w), lambda b, hh, i: (b, (h // nh) + hh)),
            pl.BlockSpec((seq, w), lambda b, hh, i: (b, 2 * (h // nh) + hh)),
            pl.BlockSpec((None, nh, 1, seq), lambda b, hh, i: (b, hh, 0, 0)),
        ],
        out_specs=pl.BlockSpec((blk, w), lambda b, hh, i: (b * nblk + i, hh)),
        out_shape=jax.ShapeDtypeStruct((m, h * dh), BF16),
        scratch_shapes=[pltpu.VMEM((nh, blk, LANES), F32), pltpu.VMEM((nh, blk, dh + LANES), F32)],
        compiler_params=_params(("parallel", "parallel", "arbitrary"), flags=ATTN_FLAGS),
        name="fox_attention",
    )(proj, proj, proj, cum)


SB_LOGIT_CAP = 126.0
SB_DEAD_CARRY = SB_LOGIT_CAP + 150.0


SB_HEADS_PER_STEP = 4


def _sb_attn_kernel(q_ref, k_ref, v_ref, u_ref, o_ref, carry_scr, acc_scr, *, blk, sub, nh):
    qi = pl.program_id(2)
    q0 = pl.multiple_of(qi * blk, blk)
    nsub = blk // sub
    dh = HEAD_DIM

    def block(h, kstart, strict):
        hs = slice(h * dh, (h + 1) * dh)
        z = lax.dot_general(q_ref[:, hs], k_ref[pl.ds(kstart, blk), hs], (((1,), (1,)), ((), ())),
                            preferred_element_type=F32)
        z = jnp.minimum(z, SB_LOGIT_CAP)
        sp = jnp.log(1.0 + jnp.exp2(z)) * LOG2E
        if strict is not None:
            sp = jnp.where(strict, sp, 0.0)
        carry = carry_scr[h]
        parts = [None] * nsub
        for c in range(nsub - 1, -1, -1):
            sl = slice(c * sub, (c + 1) * sub)
            sp_c = sp[:, sl]
            g = jnp.dot(sp_c.astype(BF16), u_ref[...], preferred_element_type=F32)
            a_c = jnp.exp2(z[:, sl] - g - _rep(carry, sub // LANES))
            if strict is not None:
                a_c = jnp.where(strict[:, sl], a_c, 0.0)
            parts[c] = a_c.astype(BF16)
            carry = carry + jnp.sum(sp_c, axis=-1, keepdims=True)
        carry_scr[h] = carry
        acc_scr[:, hs] += jnp.dot(jnp.concatenate(parts, axis=1), v_ref[pl.ds(kstart, blk), hs],
                                  preferred_element_type=F32)

    carry_scr[...] = jnp.zeros_like(carry_scr)
    acc_scr[...] = jnp.zeros_like(acc_scr)
    row = lax.broadcasted_iota(jnp.int32, (blk, blk), 0)
    col = lax.broadcasted_iota(jnp.int32, (blk, blk), 1)
    for h in range(nh):
        block(h, q0, col < row)

    def alive():
        return jnp.min(carry_scr[...]) < SB_DEAD_CARRY

    def body(state):
        n, _ = state
        kstart = pl.multiple_of((qi - n) * blk, blk)
        for h in range(nh):
            block(h, kstart, None)
        return n + 1, alive()

    lax.while_loop(lambda st: jnp.logical_and(st[0] <= qi, st[1]), body, (jnp.int32(1), alive()))
    o_ref[...] = acc_scr[...].astype(o_ref.dtype)


def sb_attention(proj, *, batch, seq, blk, sub):
    m = proj.shape[0]
    h, dh, nh = SB_HEADS, HEAD_DIM, SB_HEADS_PER_STEP
    assert seq % blk == 0 and blk % sub == 0 and h % nh == 0
    nblk = seq // blk
    w = nh * dh
    u = jnp.asarray(np.tril(np.ones((sub, sub), np.float32)), BF16)
    kern = functools.partial(_sb_attn_kernel, blk=blk, sub=sub, nh=nh)
    return pl.pallas_call(
        kern, grid=(batch, h // nh, nblk),
        in_specs=[
            pl.BlockSpec((blk, w), lambda b, hh, i: (b * nblk + i, hh)),
            pl.BlockSpec((seq, w), lambda b, hh, i: (b, (h // nh) + hh)),
            pl.BlockSpec((seq, w), lambda b, hh, i: (b, 2 * (h // nh) + hh)),
            pl.BlockSpec((sub, sub), lambda b, hh, i: (0, 0)),
        ],
        out_specs=pl.BlockSpec((blk, w), lambda b, hh, i: (b * nblk + i, hh)),
        out_shape=jax.ShapeDtypeStruct((m, h * dh), BF16),
        scratch_shapes=[pltpu.VMEM((nh, blk, LANES), F32), pltpu.VMEM((blk, w), F32)],
        compiler_params=_params(("parallel", "parallel", "arbitrary")),
        name="sb_attention",
    )(proj, proj, proj, u)


LRU_TN = 3 * LANES
LRU_KW = 7 * LANES
LRU_HALO = 8


def _lru_windows(width, bd):
    starts = []
    for j in range(width // LRU_TN):
        c0, c1 = j * LRU_TN, (j + 1) * LRU_TN - 1
        r0, r1 = (c0 // bd) * bd, (c1 // bd + 1) * bd
        k0 = min((r0 // LANES) * LANES, width - LRU_KW)
        assert k0 <= r0 and r1 <= k0 + LRU_KW
        starts.append(k0)
    return starts


def _pack_lru_gates(w_r, w_i, width):
    nb, bd, _ = w_r.shape
    starts = _lru_windows(width, bd)
    eye = jnp.eye(nb, dtype=w_r.dtype)
    dense_r = jnp.einsum('nde,nm->ndme', w_r, eye).reshape(width, width)
    dense_i = jnp.einsum('nde,nm->ndme', w_i, eye).reshape(width, width)
    tiles = []
    for j, k0 in enumerate(starts):
        cs = slice(j * LRU_TN, (j + 1) * LRU_TN)
        tiles.append(jnp.concatenate([dense_r[k0:k0 + LRU_KW, cs], dense_i[k0:k0 + LRU_KW, cs]], axis=1))
    return jnp.stack(tiles).astype(BF16), starts


def _lru_kernel(gate_ref, u_ref, cw_ref, cb_ref, wri_ref, br_ref, bi_ref, lam_ref, y_ref,
                ubuf, a_scr, b_scr, h_scr, *, starts, t):
    sblk = pl.program_id(1)
    width = u_ref.shape[1]

    @pl.when(sblk == 0)
    def _():
        ubuf[0:LRU_HALO, :] = jnp.zeros((LRU_HALO, width), F32)
        h_scr[...] = jnp.zeros_like(h_scr)

    ubuf[LRU_HALO:LRU_HALO + t, :] = u_ref[...]
    nconv = cw_ref.shape[0]
    uc = cb_ref[...] + cw_ref[nconv - 1:nconv, :] * u_ref[...]
    for j in range(nconv - 1):
        off = LRU_HALO - (nconv - 1) + j
        uc = uc + cw_ref[j:j + 1, :] * ubuf[off:off + t, :]
    ubuf[0:LRU_HALO, :] = u_ref[t - LRU_HALO:t, :]

    ucb = uc.astype(BF16)
    log_sig_lam = _log_sigmoid(lam_ref[...])
    row_in_group = lax.broadcasted_iota(jnp.int32, (t, LRU_TN), 0) % SUBLANES
    for j, k0 in enumerate(starts):
        cs = slice(j * LRU_TN, (j + 1) * LRU_TN)
        pre = jnp.dot(ucb[:, k0:k0 + LRU_KW], wri_ref[j], preferred_element_type=F32)
        r = jax.nn.sigmoid(pre[:, :LRU_TN] + br_ref[:, cs])
        ig = jax.nn.sigmoid(pre[:, LRU_TN:] + bi_ref[:, cs])
        log_a = LRU_C * r * log_sig_lam[:, cs]
        a = jnp.exp(log_a)
        bterm = jnp.sqrt(1.0 - a * a) * (ig * uc[:, cs])
        d = 1
        while d < SUBLANES:
            take = row_in_group >= d
            b_sh = pltpu.roll(bterm, d, 0)
            a_sh = pltpu.roll(a, d, 0)
            bterm = jnp.where(take, a * b_sh + bterm, bterm)
            a = jnp.where(take, a * a_sh, a)
            d *= 2
        a_scr[:, cs] = a
        b_scr[:, cs] = bterm

    h_in = h_scr[...]
    for r in range(t // SUBLANES):
        rows = slice(r * SUBLANES, (r + 1) * SUBLANES)
        h = b_scr[rows, :] + a_scr[rows, :] * h_in
        b_scr[rows, :] = h
        h_in = h[SUBLANES - 1:SUBLANES, :]
    h_scr[...] = h_in
    y_ref[...] = (b_scr[...] * jax.nn.gelu(gate_ref[...], approximate=True)).astype(y_ref.dtype)


def lru_core(proj, conv_w, conv_b, wri, starts, b_r, b_i, lam, *, batch, seq, t):
    m = proj.shape[0]
    width = proj.shape[1] // 2
    nt = seq // t
    assert seq % t == 0 and t % SUBLANES == 0
    vec = lambda: pl.BlockSpec((1, width), lambda b, s: (0, 0))
    kern = functools.partial(_lru_kernel, starts=tuple(starts), t=t)
    return pl.pallas_call(
        kern, grid=(batch, nt),
        in_specs=[pl.BlockSpec((t, width), lambda b, s: (b * nt + s, 0)),
                  pl.BlockSpec((t, width), lambda b, s: (b * nt + s, 1)),
                  pl.BlockSpec(conv_w.shape, lambda b, s: (0, 0)),
                  vec(),
                  pl.BlockSpec(wri.shape, lambda b, s: (0, 0, 0)),
                  vec(), vec(), vec()],
        out_specs=pl.BlockSpec((t, width), lambda b, s: (b * nt + s, 0)),
        out_shape=jax.ShapeDtypeStruct((m, width), BF16),
        scratch_shapes=[pltpu.VMEM((t + LRU_HALO, width), F32),
                        pltpu.VMEM((t, width), F32),
                        pltpu.VMEM((t, width), F32),
                        pltpu.VMEM((1, width), F32)],
        compiler_params=_params(("parallel", "arbitrary")),
        name="lru_core",
    )(proj, proj, conv_w, conv_b.reshape(1, width), wri, b_r.reshape(1, width),
      b_i.reshape(1, width), lam.reshape(1, width))


def _col_to_row(col, eye):
    return jnp.sum(jnp.where(eye, col, 0.0), axis=0, keepdims=True)


def _mlstm_kernel(q_ref, k_ref, v_ref, o_ref, g_ref, bif_ref, hg_ref, y_ref,
                  c_scr, n_scr, m_scr, *, nheads, dk, dv):
    chunk = pl.program_id(1)
    length = q_ref.shape[0]

    @pl.when(chunk == 0)
    def _():
        c_scr[...] = jnp.zeros_like(c_scr)
        n_scr[...] = jnp.zeros_like(n_scr)
        m_scr[...] = jnp.full_like(m_scr, MLSTM_M_INIT)

    row = lax.broadcasted_iota(jnp.int32, (length, length), 0)
    col = lax.broadcasted_iota(jnp.int32, (length, length), 1)
    eye = row == col
    causal = col <= row
    g = g_ref[...] + bif_ref[...]
    lane = lax.broadcasted_iota(jnp.int32, g.shape, 1)
    kscale = dk ** -0.5

    for head in range(nheads):
        ks_ = slice(head * dk, (head + 1) * dk)
        vs_ = slice(head * dv, (head + 1) * dv)
        i_col = jnp.sum(jnp.where(lane == head, g, 0.0), axis=-1, keepdims=True)
        f_col = jnp.sum(jnp.where(lane == nheads + head, g, 0.0), axis=-1, keepdims=True)
        lf_col = _log_sigmoid(f_col)
        lf_row = _col_to_row(lf_col, eye)
        i_row = _col_to_row(i_col, eye)
        b_col = jnp.sum(jnp.where(causal, lf_row, 0.0), axis=-1, keepdims=True)
        b_row = _col_to_row(b_col, eye)
        gsum = jnp.sum(lf_row, axis=-1, keepdims=True)

        m_prev = m_scr[head]
        dm = jnp.where(causal, b_col - b_row + i_row, NEG)
        inter = b_col + m_prev
        m_t = jnp.maximum(inter, jnp.max(dm, axis=-1, keepdims=True))
        w_intra = jnp.exp(dm - m_t)
        w_inter = jnp.exp(inter - m_t)

        q = q_ref[:, ks_]
        ks = k_ref[:, ks_] * kscale
        v = v_ref[:, vs_]
        s = lax.dot_general(q, ks, (((1,), (1,)), ((), ())), preferred_element_type=F32) * w_intra
        num = (w_inter * jnp.dot(q, c_scr[head].astype(BF16), preferred_element_type=F32)
               + jnp.dot(s.astype(BF16), v, preferred_element_type=F32))
        qn = jnp.sum(q.astype(F32) * n_scr[head], axis=-1, keepdims=True)
        den = w_inter * qn + jnp.sum(s, axis=-1, keepdims=True)
        hval = num / jnp.maximum(jnp.abs(den), jnp.exp(-m_t))

        hn = hval * _rms_scale(hval) * hg_ref[:, vs_]
        y_ref[:, vs_] = (jax.nn.sigmoid(o_ref[:, vs_].astype(F32)) * hn).astype(y_ref.dtype)

        wk_col = gsum - b_col + i_col
        m_new = jnp.maximum(gsum + m_prev, jnp.max(wk_col, axis=0, keepdims=True))
        decay = jnp.exp(gsum + m_prev - m_new)
        kw = jnp.exp(wk_col - m_new) * ks.astype(F32)
        c_scr[head] = decay * c_scr[head] + lax.dot_general(
            kw.astype(BF16), v, (((0,), (0,)), ((), ())), preferred_element_type=F32)
        n_scr[head] = decay * n_scr[head] + jnp.sum(kw, axis=0, keepdims=True)
        m_scr[head] = m_new


def mlstm_core(proj, gates, b_if, head_g, *, batch, seq, chunk):
    m = proj.shape[0]
    d = head_g.shape[0]
    nh = MLSTM_HEADS
    dk, dv = d // 2 // nh, d // nh
    qk = nh * dk
    nc = seq // chunk
    ng = gates.shape[1]
    bif = jnp.zeros((1, ng), F32).at[0, :2 * nh].set(b_if.reshape(-1))
    kern = functools.partial(_mlstm_kernel, nheads=nh, dk=dk, dv=dv)
    return pl.pallas_call(
        kern, grid=(batch, nc),
        in_specs=[pl.BlockSpec((chunk, qk), lambda b, c: (b * nc + c, 0)),
                  pl.BlockSpec((chunk, qk), lambda b, c: (b * nc + c, 1)),
                  pl.BlockSpec((chunk, d), lambda b, c: (b * nc + c, (2 * qk) // d)),
                  pl.BlockSpec((chunk, d), lambda b, c: (b * nc + c, (2 * qk) // d + 1)),
                  pl.BlockSpec((chunk, ng), lambda b, c: (b * nc + c, 0)),
                  pl.BlockSpec((1, ng), lambda b, c: (0, 0)),
                  pl.BlockSpec((1, d), lambda b, c: (0, 0))],
        out_specs=pl.BlockSpec((chunk, d), lambda b, c: (b * nc + c, 0)),
        out_shape=jax.ShapeDtypeStruct((m, d), BF16),
        scratch_shapes=[pltpu.VMEM((nh, dk, dv), F32), pltpu.VMEM((nh, 1, dk), F32),
                        pltpu.VMEM((nh, 1, 1), F32)],
        compiler_params=_params(("parallel", "arbitrary")),
        name="mlstm_core",
    )(proj, proj, proj, proj, gates, bif, head_g.reshape(1, d))


def _pad_cols(w, n):
    return jnp.pad(w, ((0, 0), (0, n - w.shape[1])))


def kernel(x, norm_g, mlp_w1, mlp_w2, fox_w_in, fox_b_f, fox_w_out, lru_w_in, lru_conv_w, lru_conv_b,
           lru_w_r, lru_b_r, lru_w_i, lru_b_i, lru_lambda, lru_w_out, sb_w_in, sb_w_out,
           mlstm_w_in, mlstm_b_if, mlstm_head_g, mlstm_w_out):
    batch, seq, d = x.shape
    depth = norm_g.shape[0]
    n_mixers = 4
    xf = x.reshape(batch * seq, d)
    tm_proj = 1024
    tm_out = 512
    attn_blk = min(512, seq)
    q_scale = HEAD_DIM ** -0.5 * LOG2E

    for i in range(depth):
        mixer, j = i % n_mixers, i // n_mixers
        if mixer == 0:
            w = fox_w_in[j]
            proj, fgate = norm_proj(xf, norm_g[i, 0], w[:, :3 * d].astype(BF16),
                                    _pad_cols(w[:, 3 * d:], LANES).astype(BF16),
                                    out_dtype=BF16, tm=tm_proj, tn=1024, q_scale=q_scale, q_cols=d)
            f_t = fgate[:, :FOX_HEADS].reshape(batch, seq, FOX_HEADS).transpose(0, 2, 1)
            cum = fox_cum(f_t, fox_b_f[j]).reshape(batch, FOX_HEADS, 1, seq)
            mixed = fox_attention(proj, cum, batch=batch, seq=seq, blk=attn_blk)
            w_out = fox_w_out[j]
        elif mixer == 1:
            width = lru_w_in.shape[2] // 2
            proj = norm_proj(xf, norm_g[i, 0], lru_w_in[j].astype(BF16), out_dtype=F32,
                             tm=tm_proj, tn=width // 3)
            wri, starts = _pack_lru_gates(lru_w_r[j], lru_w_i[j], width)
            mixed = lru_core(proj, lru_conv_w[j], lru_conv_b[j], wri, starts, lru_b_r[j], lru_b_i[j],
                             lru_lambda[j], batch=batch, seq=seq, t=min(256, seq))
            w_out = lru_w_out[j]
        elif mixer == 2:
            proj = norm_proj(xf, norm_g[i, 0], sb_w_in[j].astype(BF16), out_dtype=BF16,
                             tm=tm_proj, tn=1024, q_scale=q_scale, q_cols=d)
            mixed = sb_attention(proj, batch=batch, seq=seq, blk=attn_blk, sub=256)
            w_out = sb_w_out[j]
        else:
            w = mlstm_w_in[j]
            proj, gates = norm_proj(xf, norm_g[i, 0], w[:, :3 * d].astype(BF16),
                                    _pad_cols(w[:, 3 * d:], LANES).astype(BF16),
                                    out_dtype=BF16, tm=tm_proj, tn=1024)
            mixed = mlstm_core(proj, gates, mlstm_b_if[j], mlstm_head_g[j],
                               batch=batch, seq=seq, chunk=min(256, seq))
            w_out = mlstm_w_out[j]
        xf = proj_norm_res(mixed, w_out.astype(BF16), norm_g[i, 1], xf, tm=tm_out)
        xf = mlp_block(xf, norm_g[i, 2], mlp_w1[i].astype(BF16), mlp_w2[i].astype(BF16),
                       norm_g[i, 3], tm=512, tf=1024)
    return xf.reshape(batch, seq, d)
```

```python
import functools
import math

import numpy as np
import jax
import jax.numpy as jnp
from jax import lax
from jax.experimental import pallas as pl
from jax.experimental.pallas import tpu as pltpu

F32 = jnp.float32
BF16 = jnp.bfloat16

EPS = 1e-6
LANES = 128
SUBLANES = 8
VMEM_LIMIT = 56 * 1024 * 1024
NEG = -1e30
LOG2E = math.log2(math.e)

FOX_HEADS = 16
SB_HEADS = 16
HEAD_DIM = 128
LRU_BLOCKS = 16
LRU_C = 8.0
MLSTM_HEADS = 4
MLSTM_M_INIT = -1e30


def _params(sem, vmem=VMEM_LIMIT, flags=None):
    return pltpu.CompilerParams(dimension_semantics=sem, vmem_limit_bytes=vmem, flags=flags)


def _log_sigmoid(x):
    return jnp.minimum(x, 0.0) - jnp.log1p(jnp.exp(-jnp.abs(x)))


def _rms_scale(x):
    return lax.rsqrt(jnp.mean(x * x, axis=-1, keepdims=True) + EPS)


def _project(h_scr, w_ref, o_ref, q_scale, q_tiles):
    acc = jnp.dot(h_scr[...], w_ref[...], preferred_element_type=F32)
    if q_tiles:
        acc = acc * jnp.where(pl.program_id(1) < q_tiles, q_scale, 1.0)
    o_ref[...] = acc.astype(o_ref.dtype)


def _norm_proj_kernel(x_ref, g_ref, w_ref, o_ref, h_scr, *, q_scale, q_tiles):
    @pl.when(pl.program_id(1) == 0)
    def _():
        x = x_ref[...]
        h_scr[...] = (x * _rms_scale(x) * g_ref[...]).astype(BF16)

    _project(h_scr, w_ref, o_ref, q_scale, q_tiles)


def _norm_proj_gate_kernel(x_ref, g_ref, w_ref, wg_ref, o_ref, og_ref, h_scr, *, q_scale, q_tiles):
    @pl.when(pl.program_id(1) == 0)
    def _():
        x = x_ref[...]
        h = (x * _rms_scale(x) * g_ref[...]).astype(BF16)
        h_scr[...] = h
        og_ref[...] = jnp.dot(h, wg_ref[...], preferred_element_type=F32)

    _project(h_scr, w_ref, o_ref, q_scale, q_tiles)


def norm_proj(x, g, w, wg=None, *, out_dtype, tm, tn, q_scale=1.0, q_cols=0):
    m, d = x.shape
    n = w.shape[1]
    assert m % tm == 0 and n % tn == 0 and q_cols % tn == 0
    grid = (m // tm, n // tn)
    x_spec = pl.BlockSpec((tm, d), lambda i, j: (i, 0))
    g_spec = pl.BlockSpec((1, d), lambda i, j: (0, 0))
    w_spec = pl.BlockSpec((d, tn), lambda i, j: (0, j))
    o_spec = pl.BlockSpec((tm, tn), lambda i, j: (i, j))
    scratch = [pltpu.VMEM((tm, d), BF16)]
    statics = dict(q_scale=q_scale, q_tiles=q_cols // tn)
    if wg is None:
        return pl.pallas_call(
            functools.partial(_norm_proj_kernel, **statics), grid=grid,
            in_specs=[x_spec, g_spec, w_spec], out_specs=o_spec,
            out_shape=jax.ShapeDtypeStruct((m, n), out_dtype),
            scratch_shapes=scratch,
            compiler_params=_params(("parallel", "arbitrary")),
            name="norm_proj",
        )(x, g.reshape(1, d), w)
    ng = wg.shape[1]
    return pl.pallas_call(
        functools.partial(_norm_proj_gate_kernel, **statics), grid=grid,
        in_specs=[x_spec, g_spec, w_spec, pl.BlockSpec((d, ng), lambda i, j: (0, 0))],
        out_specs=[o_spec, pl.BlockSpec((tm, ng), lambda i, j: (i, 0))],
        out_shape=[jax.ShapeDtypeStruct((m, n), out_dtype), jax.ShapeDtypeStruct((m, ng), F32)],
        scratch_shapes=scratch,
        compiler_params=_params(("parallel", "arbitrary")),
        name="norm_proj_gate",
    )(x, g.reshape(1, d), w, wg)


def _proj_norm_res_kernel(a_ref, w_ref, g_ref, x_ref, o_ref):
    y = jnp.dot(a_ref[...], w_ref[...], preferred_element_type=F32)
    o_ref[...] = x_ref[...] + y * _rms_scale(y) * g_ref[...]


def proj_norm_res(a, w, g, x, *, tm):
    m, k = a.shape
    d = w.shape[1]
    assert m % tm == 0
    return pl.pallas_call(
        _proj_norm_res_kernel, grid=(m // tm,),
        in_specs=[pl.BlockSpec((tm, k), lambda i: (i, 0)),
                  pl.BlockSpec((k, d), lambda i: (0, 0)),
                  pl.BlockSpec((1, d), lambda i: (0, 0)),
                  pl.BlockSpec((tm, d), lambda i: (i, 0))],
        out_specs=pl.BlockSpec((tm, d), lambda i: (i, 0)),
        out_shape=jax.ShapeDtypeStruct((m, d), F32),
        compiler_params=_params(("parallel",)),
        name="proj_norm_res",
    )(a, w, g.reshape(1, d), x)


def _mlp_kernel(x_ref, g2_ref, w1_ref, w2_ref, g3_ref, o_ref, h_scr, acc_scr):
    f = pl.program_id(1)

    @pl.when(f == 0)
    def _():
        x = x_ref[...]
        h_scr[...] = (x * _rms_scale(x) * g2_ref[...]).astype(BF16)
        acc_scr[...] = jnp.zeros_like(acc_scr)

    a = jnp.dot(h_scr[...], w1_ref[...], preferred_element_type=F32)
    a = jnp.square(jnp.maximum(a, 0.0)).astype(BF16)
    acc_scr[...] += jnp.dot(a, w2_ref[...], preferred_element_type=F32)

    @pl.when(f == pl.num_programs(1) - 1)
    def _():
        y = acc_scr[...]
        o_ref[...] = x_ref[...] + y * _rms_scale(y) * g3_ref[...]


def mlp_block(x, g2, w1, w2, g3, *, tm, tf):
    m, d = x.shape
    ff = w1.shape[1]
    assert m % tm == 0 and ff % tf == 0
    return pl.pallas_call(
        _mlp_kernel, grid=(m // tm, ff // tf),
        in_specs=[pl.BlockSpec((tm, d), lambda i, f: (i, 0)),
                  pl.BlockSpec((1, d), lambda i, f: (0, 0)),
                  pl.BlockSpec((d, tf), lambda i, f: (0, f)),
                  pl.BlockSpec((tf, d), lambda i, f: (f, 0)),
                  pl.BlockSpec((1, d), lambda i, f: (0, 0))],
        out_specs=pl.BlockSpec((tm, d), lambda i, f: (i, 0)),
        out_shape=jax.ShapeDtypeStruct((m, d), F32),
        scratch_shapes=[pltpu.VMEM((tm, d), BF16), pltpu.VMEM((tm, d), F32)],
        compiler_params=_params(("parallel", "arbitrary")),
        name="mlp_block",
    )(x, g2.reshape(1, d), w1, w2, g3.reshape(1, d))


def _fox_cum_kernel(f_ref, b_ref, c_ref, carry_scr):
    nblk = f_ref.shape[1] // LANES
    lane = lax.broadcasted_iota(jnp.int32, (f_ref.shape[0], LANES), 1)
    carry_scr[...] = jnp.zeros_like(carry_scr)

    def body(j, _):
        off = pl.multiple_of(j * LANES, LANES)
        v = _log_sigmoid(f_ref[:, pl.ds(off, LANES)] + b_ref[...])
        d = 1
        while d < LANES:
            v = v + jnp.where(lane >= d, pltpu.roll(v, d, 1), 0.0)
            d *= 2
        v = v + carry_scr[...]
        c_ref[:, pl.ds(off, LANES)] = v * LOG2E
        carry_scr[...] = v[:, LANES - 1:LANES]
        return 0

    lax.fori_loop(0, nblk, body, 0)


def fox_cum(f_t, b_f):
    b, h, s = f_t.shape
    return pl.pallas_call(
        _fox_cum_kernel, grid=(b,),
        in_specs=[pl.BlockSpec((None, h, s), lambda i: (i, 0, 0)),
                  pl.BlockSpec((h, 1), lambda i: (0, 0))],
        out_specs=pl.BlockSpec((None, h, s), lambda i: (i, 0, 0)),
        out_shape=jax.ShapeDtypeStruct((b, h, s), F32),
        scratch_shapes=[pltpu.VMEM((h, 1), F32)],
        compiler_params=_params(("parallel",)),
        name="fox_cum",
    )(f_t, b_f.reshape(h, 1))


ATTN_FLAGS = None


def _rep(x, n):
    return jnp.concatenate([x] * n, axis=1) if n > 1 else x


FOX_HEADS_PER_STEP = 4


FOX_DEAD_MARGIN = 152.0


def _fox_attn_kernel(q_ref, k_ref, v_ref, c_ref, o_ref, m_scr, acc_scr, knorm_scr, *, blk, nh):
    qi = pl.program_id(2)
    q0 = pl.multiple_of(qi * blk, blk)
    dh = HEAD_DIM
    ones = jnp.ones((blk, LANES), BF16)

    def block(h, kstart, causal):
        hs = slice(h * dh, (h + 1) * dh)
        s = lax.dot_general(q_ref[:, hs], k_ref[pl.ds(kstart, blk), hs], (((1,), (1,)), ((), ())),
                            preferred_element_type=F32)
        c0 = c_ref[h, :, pl.ds(q0, LANES)][:, 0:1]
        s = s + (c0 - c_ref[h, :, pl.ds(kstart, blk)])
        if causal is not None:
            s = jnp.where(causal, s, NEG)
        m_prev = m_scr[h]
        m_new = jnp.maximum(m_prev, jnp.max(s, axis=-1, keepdims=True))
        p = jnp.exp2(s - _rep(m_new, blk // LANES)).astype(BF16)
        alpha = jnp.exp2(m_prev - m_new)
        v = jnp.concatenate([v_ref[pl.ds(kstart, blk), hs], ones], axis=1)
        acc_scr[h] = _rep(alpha, 2) * acc_scr[h] + jnp.dot(p, v, preferred_element_type=F32)
        m_scr[h] = m_new

    @pl.when(qi == 0)
    def _():
        for h in range(nh):
            kk = k_ref[:, h * dh:(h + 1) * dh].astype(F32)
            knorm_scr[h] = jnp.sqrt(jnp.max(jnp.sum(kk * kk, axis=-1, keepdims=True), axis=0, keepdims=True))

    m_scr[...] = jnp.full_like(m_scr, NEG)
    acc_scr[...] = jnp.zeros_like(acc_scr)
    row = lax.broadcasted_iota(jnp.int32, (blk, blk), 0)
    col = lax.broadcasted_iota(jnp.int32, (blk, blk), 1)
    for h in range(nh):
        block(h, q0, col <= row)

    reach, c_start = [], []
    for h in range(nh):
        qq = q_ref[:, h * dh:(h + 1) * dh].astype(F32)
        qnorm = jnp.sqrt(jnp.max(jnp.sum(qq * qq, axis=-1, keepdims=True), axis=0, keepdims=True))
        reach.append(qnorm * knorm_scr[h])
        c_start.append(c_ref[h, :, pl.ds(q0, LANES)][:, 0:1])

    def block_start(n):
        return pl.multiple_of(jnp.maximum(qi - n, 0) * blk, blk)

    def alive(n):
        last_chunk = block_start(n) + (blk - LANES)
        live = False
        for h in range(nh):
            c_end = c_ref[h, :, pl.ds(last_chunk, LANES)][:, LANES - 1:LANES]
            bound = jnp.max(reach[h] + (c_start[h] - c_end))
            live = jnp.logical_or(live, bound - jnp.min(m_scr[h]) > -FOX_DEAD_MARGIN)
        return live

    def body(state):
        n, _ = state
        kstart = block_start(n)
        for h in range(nh):
            block(h, kstart, None)
        return n + 1, alive(n + 1)

    lax.while_loop(lambda st: jnp.logical_and(st[0] <= qi, st[1]), body, (jnp.int32(1), alive(1)))
    for h in range(nh):
        total = acc_scr[h]
        o_ref[:, h * dh:(h + 1) * dh] = (total[:, :dh] / total[:, dh:]).astype(o_ref.dtype)


def fox_attention(proj, cum, *, batch, seq, blk):
    m = proj.shape[0]
    h, dh, nh = FOX_HEADS, HEAD_DIM, FOX_HEADS_PER_STEP
    assert dh == LANES and seq % blk == 0 and h % nh == 0
    nblk = seq // blk
    w = nh * dh
    kern = functools.partial(_fox_attn_kernel, blk=blk, nh=nh)
    return pl.pallas_call(
        kern, grid=(batch, h // nh, nblk),
        in_specs=[
            pl.BlockSpec((blk, w), lambda b, hh, i: (b * nblk + i, hh)),
            pl.BlockSpec((seq, w), lambda b, hh, i: (b, (h // nh) + hh)),
            pl.BlockSpec((seq, w), lambda b, hh, i: (b, 2 * (h // nh) + hh)),
            pl.BlockSpec((None, nh, 1, seq), lambda b, hh, i: (b, hh, 0, 0)),
        ],
        out_specs=pl.BlockSpec((blk, w), lambda b, hh, i: (b * nblk + i, hh)),
        out_shape=jax.ShapeDtypeStruct((m, h * dh), BF16),
        scratch_shapes=[pltpu.VMEM((nh, blk, LANES), F32), pltpu.VMEM((nh, blk, dh + LANES), F32),
                        pltpu.VMEM((nh, 1, 1), F32)],
        compiler_params=_params(("parallel", "parallel", "arbitrary"), flags=ATTN_FLAGS),
        name="fox_attention",
    )(proj, proj, proj, cum)


SB_LOGIT_CAP = 126.0
SB_DEAD_CARRY = SB_LOGIT_CAP + 150.0


SB_HEADS_PER_STEP = 4


def _sb_attn_kernel(q_ref, k_ref, v_ref, u_ref, o_ref, carry_scr, acc_scr, *, blk, sub, nh):
    qi = pl.program_id(2)
    q0 = pl.multiple_of(qi * blk, blk)
    nsub = blk // sub
    dh = HEAD_DIM

    def block(h, kstart, strict):
        hs = slice(h * dh, (h + 1) * dh)
        z = lax.dot_general(q_ref[:, hs], k_ref[pl.ds(kstart, blk), hs], (((1,), (1,)), ((), ())),
                            preferred_element_type=F32)
        z = jnp.minimum(z, SB_LOGIT_CAP)
        sp = jnp.log(1.0 + jnp.exp2(z)) * LOG2E
        if strict is not None:
            sp = jnp.where(strict, sp, 0.0)
        carry = carry_scr[h]
        parts = [None] * nsub
        for c in range(nsub - 1, -1, -1):
            sl = slice(c * sub, (c + 1) * sub)
            sp_c = sp[:, sl]
            g = jnp.dot(sp_c.astype(BF16), u_ref[...], preferred_element_type=F32)
            a_c = jnp.exp2(z[:, sl] - g - _rep(carry, sub // LANES))
            if strict is not None:
                a_c = jnp.where(strict[:, sl], a_c, 0.0)
            parts[c] = a_c.astype(BF16)
            carry = carry + jnp.sum(sp_c, axis=-1, keepdims=True)
        carry_scr[h] = carry
        acc_scr[:, hs] += jnp.dot(jnp.concatenate(parts, axis=1), v_ref[pl.ds(kstart, blk), hs],
                                  preferred_element_type=F32)

    carry_scr[...] = jnp.zeros_like(carry_scr)
    acc_scr[...] = jnp.zeros_like(acc_scr)
    row = lax.broadcasted_iota(jnp.int32, (blk, blk), 0)
    col = lax.broadcasted_iota(jnp.int32, (blk, blk), 1)
    for h in range(nh):
        block(h, q0, col < row)

    def alive():
        return jnp.min(carry_scr[...]) < SB_DEAD_CARRY

    def body(state):
        n, _ = state
        kstart = pl.multiple_of((qi - n) * blk, blk)
        for h in range(nh):
            block(h, kstart, None)
        return n + 1, alive()

    lax.while_loop(lambda st: jnp.logical_and(st[0] <= qi, st[1]), body, (jnp.int32(1), alive()))
    o_ref[...] = acc_scr[...].astype(o_ref.dtype)


def sb_attention(proj, *, batch, seq, blk, sub):
    m = proj.shape[0]
    h, dh, nh = SB_HEADS, HEAD_DIM, SB_HEADS_PER_STEP
    assert seq % blk == 0 and blk % sub == 0 and h % nh == 0
    nblk = seq // blk
    w = nh * dh
    u = jnp.asarray(np.tril(np.ones((sub, sub), np.float32)), BF16)
    kern = functools.partial(_sb_attn_kernel, blk=blk, sub=sub, nh=nh)
    return pl.pallas_call(
        kern, grid=(batch, h // nh, nblk),
        in_specs=[
            pl.BlockSpec((blk, w), lambda b, hh, i: (b * nblk + i, hh)),
            pl.BlockSpec((seq, w), lambda b, hh, i: (b, (h // nh) + hh)),
            pl.BlockSpec((seq, w), lambda b, hh, i: (b, 2 * (h // nh) + hh)),
            pl.BlockSpec((sub, sub), lambda b, hh, i: (0, 0)),
        ],
        out_specs=pl.BlockSpec((blk, w), lambda b, hh, i: (b * nblk + i, hh)),
        out_shape=jax.ShapeDtypeStruct((m, h * dh), BF16),
        scratch_shapes=[pltpu.VMEM((nh, blk, LANES), F32), pltpu.VMEM((blk, w), F32)],
        compiler_params=_params(("parallel", "parallel", "arbitrary")),
        name="sb_attention",
    )(proj, proj, proj, u)


LRU_TN = 3 * LANES
LRU_KW = 7 * LANES
LRU_HALO = 8


def _lru_windows(width, bd):
    starts = []
    for j in range(width // LRU_TN):
        c0, c1 = j * LRU_TN, (j + 1) * LRU_TN - 1
        r0, r1 = (c0 // bd) * bd, (c1 // bd + 1) * bd
        k0 = min((r0 // LANES) * LANES, width - LRU_KW)
        assert k0 <= r0 and r1 <= k0 + LRU_KW
        starts.append(k0)
    return starts


def _pack_lru_gates(w_r, w_i, width):
    nb, bd, _ = w_r.shape
    starts = _lru_windows(width, bd)
    eye = jnp.eye(nb, dtype=w_r.dtype)
    dense_r = jnp.einsum('nde,nm->ndme', w_r, eye).reshape(width, width)
    dense_i = jnp.einsum('nde,nm->ndme', w_i, eye).reshape(width, width)
    tiles = []
    for j, k0 in enumerate(starts):
        cs = slice(j * LRU_TN, (j + 1) * LRU_TN)
        tiles.append(jnp.concatenate([dense_r[k0:k0 + LRU_KW, cs], dense_i[k0:k0 + LRU_KW, cs]], axis=1))
    return jnp.stack(tiles).astype(BF16), starts


def _lru_kernel(gate_ref, u_ref, cw_ref, cb_ref, wri_ref, br_ref, bi_ref, lam_ref, y_ref,
                ubuf, a_scr, b_scr, h_scr, *, starts, t):
    sblk = pl.program_id(1)
    width = u_ref.shape[1]

    @pl.when(sblk == 0)
    def _():
        ubuf[0:LRU_HALO, :] = jnp.zeros((LRU_HALO, width), F32)
        h_scr[...] = jnp.zeros_like(h_scr)

    ubuf[LRU_HALO:LRU_HALO + t, :] = u_ref[...]
    nconv = cw_ref.shape[0]
    uc = cb_ref[...] + cw_ref[nconv - 1:nconv, :] * u_ref[...]
    for j in range(nconv - 1):
        off = LRU_HALO - (nconv - 1) + j
        uc = uc + cw_ref[j:j + 1, :] * ubuf[off:off + t, :]
    ubuf[0:LRU_HALO, :] = u_ref[t - LRU_HALO:t, :]

    ucb = uc.astype(BF16)
    log_sig_lam = _log_sigmoid(lam_ref[...])
    row_in_group = lax.broadcasted_iota(jnp.int32, (t, LRU_TN), 0) % SUBLANES
    for j, k0 in enumerate(starts):
        cs = slice(j * LRU_TN, (j + 1) * LRU_TN)
        pre = jnp.dot(ucb[:, k0:k0 + LRU_KW], wri_ref[j], preferred_element_type=F32)
        r = jax.nn.sigmoid(pre[:, :LRU_TN] + br_ref[:, cs])
        ig = jax.nn.sigmoid(pre[:, LRU_TN:] + bi_ref[:, cs])
        log_a = LRU_C * r * log_sig_lam[:, cs]
        a = jnp.exp(log_a)
        bterm = jnp.sqrt(1.0 - a * a) * (ig * uc[:, cs])
        d = 1
        while d < SUBLANES:
            take = row_in_group >= d
            b_sh = pltpu.roll(bterm, d, 0)
            a_sh = pltpu.roll(a, d, 0)
            bterm = jnp.where(take, a * b_sh + bterm, bterm)
            a = jnp.where(take, a * a_sh, a)
            d *= 2
        a_scr[:, cs] = a
        b_scr[:, cs] = bterm

    h_in = h_scr[...]
    for r in range(t // SUBLANES):
        rows = slice(r * SUBLANES, (r + 1) * SUBLANES)
        h = b_scr[rows, :] + a_scr[rows, :] * h_in
        b_scr[rows, :] = h
        h_in = h[SUBLANES - 1:SUBLANES, :]
    h_scr[...] = h_in
    y_ref[...] = (b_scr[...] * jax.nn.gelu(gate_ref[...], approximate=True)).astype(y_ref.dtype)


def lru_core(proj, conv_w, conv_b, wri, starts, b_r, b_i, lam, *, batch, seq, t):
    m = proj.shape[0]
    width = proj.shape[1] // 2
    nt = seq // t
    assert seq % t == 0 and t % SUBLANES == 0
    vec = lambda: pl.BlockSpec((1, width), lambda b, s: (0, 0))
    kern = functools.partial(_lru_kernel, starts=tuple(starts), t=t)
    return pl.pallas_call(
        kern, grid=(batch, nt),
        in_specs=[pl.BlockSpec((t, width), lambda b, s: (b * nt + s, 0)),
                  pl.BlockSpec((t, width), lambda b, s: (b * nt + s, 1)),
                  pl.BlockSpec(conv_w.shape, lambda b, s: (0, 0)),
                  vec(),
                  pl.BlockSpec(wri.shape, lambda b, s: (0, 0, 0)),
                  vec(), vec(), vec()],
        out_specs=pl.BlockSpec((t, width), lambda b, s: (b * nt + s, 0)),
        out_shape=jax.ShapeDtypeStruct((m, width), BF16),
        scratch_shapes=[pltpu.VMEM((t + LRU_HALO, width), F32),
                        pltpu.VMEM((t, width), F32),
                        pltpu.VMEM((t, width), F32),
                        pltpu.VMEM((1, width), F32)],
        compiler_params=_params(("parallel", "arbitrary")),
        name="lru_core",
    )(proj, proj, conv_w, conv_b.reshape(1, width), wri, b_r.reshape(1, width),
      b_i.reshape(1, width), lam.reshape(1, width))


def _col_to_row(col, eye):
    return jnp.sum(jnp.where(eye, col, 0.0), axis=0, keepdims=True)


def _mlstm_kernel(q_ref, k_ref, v_ref, o_ref, g_ref, bif_ref, hg_ref, y_ref,
                  c_scr, n_scr, m_scr, *, nheads, dk, dv):
    chunk = pl.program_id(1)
    length = q_ref.shape[0]

    @pl.when(chunk == 0)
    def _():
        c_scr[...] = jnp.zeros_like(c_scr)
        n_scr[...] = jnp.zeros_like(n_scr)
        m_scr[...] = jnp.full_like(m_scr, MLSTM_M_INIT)

    row = lax.broadcasted_iota(jnp.int32, (length, length), 0)
    col = lax.broadcasted_iota(jnp.int32, (length, length), 1)
    eye = row == col
    causal = col <= row
    g = g_ref[...] + bif_ref[...]
    lane = lax.broadcasted_iota(jnp.int32, g.shape, 1)
    kscale = dk ** -0.5

    for head in range(nheads):
        ks_ = slice(head * dk, (head + 1) * dk)
        vs_ = slice(head * dv, (head + 1) * dv)
        i_col = jnp.sum(jnp.where(lane == head, g, 0.0), axis=-1, keepdims=True)
        f_col = jnp.sum(jnp.where(lane == nheads + head, g, 0.0), axis=-1, keepdims=True)
        lf_col = _log_sigmoid(f_col)
        lf_row = _col_to_row(lf_col, eye)
        i_row = _col_to_row(i_col, eye)
        b_col = jnp.sum(jnp.where(causal, lf_row, 0.0), axis=-1, keepdims=True)
        b_row = _col_to_row(b_col, eye)
        gsum = jnp.sum(lf_row, axis=-1, keepdims=True)

        m_prev = m_scr[head]
        dm = jnp.where(causal, b_col - b_row + i_row, NEG)
        inter = b_col + m_prev
        m_t = jnp.maximum(inter, jnp.max(dm, axis=-1, keepdims=True))
        w_intra = jnp.exp(dm - m_t)
        w_inter = jnp.exp(inter - m_t)

        q = q_ref[:, ks_]
        ks = k_ref[:, ks_] * kscale
        v = v_ref[:, vs_]
        s = lax.dot_general(q, ks, (((1,), (1,)), ((), ())), preferred_element_type=F32) * w_intra
        num = (w_inter * jnp.dot(q, c_scr[head].astype(BF16), preferred_element_type=F32)
               + jnp.dot(s.astype(BF16), v, preferred_element_type=F32))
        qn = jnp.sum(q.astype(F32) * n_scr[head], axis=-1, keepdims=True)
        den = w_inter * qn + jnp.sum(s, axis=-1, keepdims=True)
        hval = num / jnp.maximum(jnp.abs(den), jnp.exp(-m_t))

        hn = hval * _rms_scale(hval) * hg_ref[:, vs_]
        y_ref[:, vs_] = (jax.nn.sigmoid(o_ref[:, vs_].astype(F32)) * hn).astype(y_ref.dtype)

        wk_col = gsum - b_col + i_col
        m_new = jnp.maximum(gsum + m_prev, jnp.max(wk_col, axis=0, keepdims=True))
        decay = jnp.exp(gsum + m_prev - m_new)
        kw = jnp.exp(wk_col - m_new) * ks.astype(F32)
        c_scr[head] = decay * c_scr[head] + lax.dot_general(
            kw.astype(BF16), v, (((0,), (0,)), ((), ())), preferred_element_type=F32)
        n_scr[head] = decay * n_scr[head] + jnp.sum(kw, axis=0, keepdims=True)
        m_scr[head] = m_new


def mlstm_core(proj, gates, b_if, head_g, *, batch, seq, chunk):
    m = proj.shape[0]
    d = head_g.shape[0]
    nh = MLSTM_HEADS
    dk, dv = d // 2 // nh, d // nh
    qk = nh * dk
    nc = seq // chunk
    ng = gates.shape[1]
    bif = jnp.zeros((1, ng), F32).at[0, :2 * nh].set(b_if.reshape(-1))
    kern = functools.partial(_mlstm_kernel, nheads=nh, dk=dk, dv=dv)
    return pl.pallas_call(
        kern, grid=(batch, nc),
        in_specs=[pl.BlockSpec((chunk, qk), lambda b, c: (b * nc + c, 0)),
                  pl.BlockSpec((chunk, qk), lambda b, c: (b * nc + c, 1)),
                  pl.BlockSpec((chunk, d), lambda b, c: (b * nc + c, (2 * qk) // d)),
                  pl.BlockSpec((chunk, d), lambda b, c: (b * nc + c, (2 * qk) // d + 1)),
                  pl.BlockSpec((chunk, ng), lambda b, c: (b * nc + c, 0)),
                  pl.BlockSpec((1, ng), lambda b, c: (0, 0)),
                  pl.BlockSpec((1, d), lambda b, c: (0, 0))],
        out_specs=pl.BlockSpec((chunk, d), lambda b, c: (b * nc + c, 0)),
        out_shape=jax.ShapeDtypeStruct((m, d), BF16),
        scratch_shapes=[pltpu.VMEM((nh, dk, dv), F32), pltpu.VMEM((nh, 1, dk), F32),
                        pltpu.VMEM((nh, 1, 1), F32)],
        compiler_params=_params(("parallel", "arbitrary")),
        name="mlstm_core",
    )(proj, proj, proj, proj, gates, bif, head_g.reshape(1, d))


def _pad_cols(w, n):
    return jnp.pad(w, ((0, 0), (0, n - w.shape[1])))


def kernel(x, norm_g, mlp_w1, mlp_w2, fox_w_in, fox_b_f, fox_w_out, lru_w_in, lru_conv_w, lru_conv_b,
           lru_w_r, lru_b_r, lru_w_i, lru_b_i, lru_lambda, lru_w_out, sb_w_in, sb_w_out,
           mlstm_w_in, mlstm_b_if, mlstm_head_g, mlstm_w_out):
    batch, seq, d = x.shape
    depth = norm_g.shape[0]
    n_mixers = 4
    xf = x.reshape(batch * seq, d)
    tm_proj = 1024
    tm_out = 512
    attn_blk = min(512, seq)
    q_scale = HEAD_DIM ** -0.5 * LOG2E

    for i in range(depth):
        mixer, j = i % n_mixers, i // n_mixers
        if mixer == 0:
            w = fox_w_in[j]
            proj, fgate = norm_proj(xf, norm_g[i, 0], w[:, :3 * d].astype(BF16),
                                    _pad_cols(w[:, 3 * d:], LANES).astype(BF16),
                                    out_dtype=BF16, tm=tm_proj, tn=1024, q_scale=q_scale, q_cols=d)
            f_t = fgate[:, :FOX_HEADS].reshape(batch, seq, FOX_HEADS).transpose(0, 2, 1)
            cum = fox_cum(f_t, fox_b_f[j]).reshape(batch, FOX_HEADS, 1, seq)
            mixed = fox_attention(proj, cum, batch=batch, seq=seq, blk=attn_blk)
            w_out = fox_w_out[j]
        elif mixer == 1:
            width = lru_w_in.shape[2] // 2
            proj = norm_proj(xf, norm_g[i, 0], lru_w_in[j].astype(BF16), out_dtype=F32,
                             tm=tm_proj, tn=width // 3)
            wri, starts = _pack_lru_gates(lru_w_r[j], lru_w_i[j], width)
            mixed = lru_core(proj, lru_conv_w[j], lru_conv_b[j], wri, starts, lru_b_r[j], lru_b_i[j],
                             lru_lambda[j], batch=batch, seq=seq, t=min(256, seq))
            w_out = lru_w_out[j]
        elif mixer == 2:
            proj = norm_proj(xf, norm_g[i, 0], sb_w_in[j].astype(BF16), out_dtype=BF16,
                             tm=tm_proj, tn=1024, q_scale=q_scale, q_cols=d)
            mixed = sb_attention(proj, batch=batch, seq=seq, blk=attn_blk, sub=256)
            w_out = sb_w_out[j]
        else:
            w = mlstm_w_in[j]
            proj, gates = norm_proj(xf, norm_g[i, 0], w[:, :3 * d].astype(BF16),
                                    _pad_cols(w[:, 3 * d:], LANES).astype(BF16),
                                    out_dtype=BF16, tm=tm_proj, tn=1024)
            mixed = mlstm_core(proj, gates, mlstm_b_if[j], mlstm_head_g[j],
                               batch=batch, seq=seq, chunk=min(256, seq))
            w_out = mlstm_w_out[j]
        xf = proj_norm_res(mixed, w_out.astype(BF16), norm_g[i, 1], xf, tm=tm_out)
        xf = mlp_block(xf, norm_g[i, 2], mlp_w1[i].astype(BF16), mlp_w2[i].astype(BF16),
                       norm_g[i, 3], tm=512, tf=1024)
    return xf.reshape(batch, seq, d)
```

```python
import functools
import math

import numpy as np
import jax
import jax.numpy as jnp
from jax import lax
from jax.experimental import pallas as pl
from jax.experimental.pallas import tpu as pltpu

F32 = jnp.float32
BF16 = jnp.bfloat16

EPS = 1e-6
LANES = 128
SUBLANES = 8
VMEM_LIMIT = 56 * 1024 * 1024
NEG = -1e30
LOG2E = math.log2(math.e)

FOX_HEADS = 16
SB_HEADS = 16
HEAD_DIM = 128
LRU_BLOCKS = 16
LRU_C = 8.0
MLSTM_HEADS = 4
MLSTM_M_INIT = -1e30


def _params(sem, vmem=VMEM_LIMIT, flags=None):
    return pltpu.CompilerParams(dimension_semantics=sem, vmem_limit_bytes=vmem, flags=flags)


def _log_sigmoid(x):
    return jnp.minimum(x, 0.0) - jnp.log1p(jnp.exp(-jnp.abs(x)))


def _rms_scale(x):
    return lax.rsqrt(jnp.mean(x * x, axis=-1, keepdims=True) + EPS)


def _project(h_scr, w_ref, o_ref, q_scale, q_tiles):
    acc = jnp.dot(h_scr[...], w_ref[...], preferred_element_type=F32)
    if q_tiles:
        acc = acc * jnp.where(pl.program_id(1) < q_tiles, q_scale, 1.0)
    o_ref[...] = acc.astype(o_ref.dtype)


def _norm_proj_kernel(x_ref, g_ref, w_ref, o_ref, h_scr, *, q_scale, q_tiles):
    @pl.when(pl.program_id(1) == 0)
    def _():
        x = x_ref[...]
        h_scr[...] = (x * _rms_scale(x) * g_ref[...]).astype(BF16)

    _project(h_scr, w_ref, o_ref, q_scale, q_tiles)


def _norm_proj_gate_kernel(x_ref, g_ref, w_ref, wg_ref, o_ref, og_ref, h_scr, *, q_scale, q_tiles):
    @pl.when(pl.program_id(1) == 0)
    def _():
        x = x_ref[...]
        h = (x * _rms_scale(x) * g_ref[...]).astype(BF16)
        h_scr[...] = h
        og_ref[...] = jnp.dot(h, wg_ref[...], preferred_element_type=F32)

    _project(h_scr, w_ref, o_ref, q_scale, q_tiles)


def norm_proj(x, g, w, wg=None, *, out_dtype, tm, tn, n_cols=None, q_scale=1.0, q_cols=0):
    m, d = x.shape
    n = w.shape[1] if n_cols is None else n_cols
    assert m % tm == 0 and n % tn == 0 and q_cols % tn == 0
    grid = (m // tm, n // tn)
    x_spec = pl.BlockSpec((tm, d), lambda i, j: (i, 0))
    g_spec = pl.BlockSpec((1, d), lambda i, j: (0, 0))
    w_spec = pl.BlockSpec((d, tn), lambda i, j: (0, j))
    o_spec = pl.BlockSpec((tm, tn), lambda i, j: (i, j))
    scratch = [pltpu.VMEM((tm, d), BF16)]
    statics = dict(q_scale=q_scale, q_tiles=q_cols // tn)
    if wg is None:
        return pl.pallas_call(
            functools.partial(_norm_proj_kernel, **statics), grid=grid,
            in_specs=[x_spec, g_spec, w_spec], out_specs=o_spec,
            out_shape=jax.ShapeDtypeStruct((m, n), out_dtype),
            scratch_shapes=scratch,
            compiler_params=_params(("parallel", "arbitrary")),
            name="norm_proj",
        )(x, g.reshape(1, d), w)
    ng = wg.shape[1]
    return pl.pallas_call(
        functools.partial(_norm_proj_gate_kernel, **statics), grid=grid,
        in_specs=[x_spec, g_spec, w_spec, pl.BlockSpec((d, ng), lambda i, j: (0, 0))],
        out_specs=[o_spec, pl.BlockSpec((tm, ng), lambda i, j: (i, 0))],
        out_shape=[jax.ShapeDtypeStruct((m, n), out_dtype), jax.ShapeDtypeStruct((m, ng), F32)],
        scratch_shapes=scratch,
        compiler_params=_params(("parallel", "arbitrary")),
        name="norm_proj_gate",
    )(x, g.reshape(1, d), w, wg)


def _proj_norm_res_kernel(a_ref, w_ref, g_ref, x_ref, o_ref):
    y = jnp.dot(a_ref[...], w_ref[...], preferred_element_type=F32)
    o_ref[...] = x_ref[...] + y * _rms_scale(y) * g_ref[...]


def proj_norm_res(a, w, g, x, *, tm):
    m, k = a.shape
    d = w.shape[1]
    assert m % tm == 0
    return pl.pallas_call(
        _proj_norm_res_kernel, grid=(m // tm,),
        in_specs=[pl.BlockSpec((tm, k), lambda i: (i, 0)),
                  pl.BlockSpec((k, d), lambda i: (0, 0)),
                  pl.BlockSpec((1, d), lambda i: (0, 0)),
                  pl.BlockSpec((tm, d), lambda i: (i, 0))],
        out_specs=pl.BlockSpec((tm, d), lambda i: (i, 0)),
        out_shape=jax.ShapeDtypeStruct((m, d), F32),
        compiler_params=_params(("parallel",)),
        name="proj_norm_res",
    )(a, w, g.reshape(1, d), x)


def _mlp_kernel(x_ref, g2_ref, w1_ref, w2_ref, g3_ref, o_ref, h_scr, acc_scr):
    f = pl.program_id(1)

    @pl.when(f == 0)
    def _():
        x = x_ref[...]
        h_scr[...] = (x * _rms_scale(x) * g2_ref[...]).astype(BF16)
        acc_scr[...] = jnp.zeros_like(acc_scr)

    a = jnp.dot(h_scr[...], w1_ref[...], preferred_element_type=F32)
    a = jnp.square(jnp.maximum(a, 0.0)).astype(BF16)
    acc_scr[...] += jnp.dot(a, w2_ref[...], preferred_element_type=F32)

    @pl.when(f == pl.num_programs(1) - 1)
    def _():
        y = acc_scr[...]
        o_ref[...] = x_ref[...] + y * _rms_scale(y) * g3_ref[...]


def mlp_block(x, g2, w1, w2, g3, *, tm, tf):
    m, d = x.shape
    ff = w1.shape[1]
    assert m % tm == 0 and ff % tf == 0
    return pl.pallas_call(
        _mlp_kernel, grid=(m // tm, ff // tf),
        in_specs=[pl.BlockSpec((tm, d), lambda i, f: (i, 0)),
                  pl.BlockSpec((1, d), lambda i, f: (0, 0)),
                  pl.BlockSpec((d, tf), lambda i, f: (0, f)),
                  pl.BlockSpec((tf, d), lambda i, f: (f, 0)),
                  pl.BlockSpec((1, d), lambda i, f: (0, 0))],
        out_specs=pl.BlockSpec((tm, d), lambda i, f: (i, 0)),
        out_shape=jax.ShapeDtypeStruct((m, d), F32),
        scratch_shapes=[pltpu.VMEM((tm, d), BF16), pltpu.VMEM((tm, d), F32)],
        compiler_params=_params(("parallel", "arbitrary")),
        name="mlp_block",
    )(x, g2.reshape(1, d), w1, w2, g3.reshape(1, d))


def _fox_cum_kernel(f_ref, b_ref, c_ref):
    v = _log_sigmoid(f_ref[...] + b_ref[...])
    pos = lax.broadcasted_iota(jnp.int32, v.shape, 1)
    d = 1
    while d < v.shape[1]:
        v = v + jnp.where(pos >= d, pltpu.roll(v, d, 1), 0.0)
        d *= 2
    c_ref[...] = v * LOG2E


def fox_cum(f_t, b_f):
    b, h, s = f_t.shape
    return pl.pallas_call(
        _fox_cum_kernel, grid=(b,),
        in_specs=[pl.BlockSpec((None, h, s), lambda i: (i, 0, 0)),
                  pl.BlockSpec((h, 1), lambda i: (0, 0))],
        out_specs=pl.BlockSpec((None, h, s), lambda i: (i, 0, 0)),
        out_shape=jax.ShapeDtypeStruct((b, h, s), F32),
        compiler_params=_params(("parallel",)),
        name="fox_cum",
    )(f_t, b_f.reshape(h, 1))


ATTN_FLAGS = None


def _rep(x, n):
    return jnp.concatenate([x] * n, axis=1) if n > 1 else x


FOX_HEADS_PER_STEP = 4


FOX_DEAD_MARGIN = 152.0


def _fox_attn_kernel(q_ref, k_ref, v_ref, c_ref, o_ref, m_scr, acc_scr, knorm_scr, *, blk, nh):
    qi = pl.program_id(2)
    q0 = pl.multiple_of(qi * blk, blk)
    dh = HEAD_DIM
    ones = jnp.ones((blk, LANES), BF16)

    def block(h, kstart, causal):
        hs = slice(h * dh, (h + 1) * dh)
        s = lax.dot_general(q_ref[:, hs], k_ref[pl.ds(kstart, blk), hs], (((1,), (1,)), ((), ())),
                            preferred_element_type=F32)
        c0 = c_ref[h, :, pl.ds(q0, LANES)][:, 0:1]
        s = s + (c0 - c_ref[h, :, pl.ds(kstart, blk)])
        if causal is not None:
            s = jnp.where(causal, s, NEG)
        m_prev = m_scr[h]
        m_new = jnp.maximum(m_prev, jnp.max(s, axis=-1, keepdims=True))
        p = jnp.exp2(s - _rep(m_new, blk // LANES)).astype(BF16)
        alpha = jnp.exp2(m_prev - m_new)
        v = jnp.concatenate([v_ref[pl.ds(kstart, blk), hs], ones], axis=1)
        acc_scr[h] = _rep(alpha, 2) * acc_scr[h] + jnp.dot(p, v, preferred_element_type=F32)
        m_scr[h] = m_new

    @pl.when(qi == 0)
    def _():
        for h in range(nh):
            kk = k_ref[:, h * dh:(h + 1) * dh].astype(F32)
            knorm_scr[h] = jnp.sqrt(jnp.max(jnp.sum(kk * kk, axis=-1, keepdims=True), axis=0, keepdims=True))

    m_scr[...] = jnp.full_like(m_scr, NEG)
    acc_scr[...] = jnp.zeros_like(acc_scr)
    row = lax.broadcasted_iota(jnp.int32, (blk, blk), 0)
    col = lax.broadcasted_iota(jnp.int32, (blk, blk), 1)
    for h in range(nh):
        block(h, q0, col <= row)

    reach, c_start = [], []
    for h in range(nh):
        qq = q_ref[:, h * dh:(h + 1) * dh].astype(F32)
        qnorm = jnp.sqrt(jnp.max(jnp.sum(qq * qq, axis=-1, keepdims=True), axis=0, keepdims=True))
        reach.append(qnorm * knorm_scr[h])
        c_start.append(c_ref[h, :, pl.ds(q0, LANES)][:, 0:1])

    def block_start(n):
        return pl.multiple_of(jnp.maximum(qi - n, 0) * blk, blk)

    def alive(n):
        last_chunk = block_start(n) + (blk - LANES)
        live = False
        for h in range(nh):
            c_end = c_ref[h, :, pl.ds(last_chunk, LANES)][:, LANES - 1:LANES]
            bound = jnp.max(reach[h] + (c_start[h] - c_end))
            live = jnp.logical_or(live, bound - jnp.min(m_scr[h]) > -FOX_DEAD_MARGIN)
        return live

    def body(state):
        n, _ = state
        kstart = block_start(n)
        for h in range(nh):
            block(h, kstart, None)
        return n + 1, alive(n + 1)

    lax.while_loop(lambda st: jnp.logical_and(st[0] <= qi, st[1]), body, (jnp.int32(1), alive(1)))
    for h in range(nh):
        total = acc_scr[h]
        o_ref[:, h * dh:(h + 1) * dh] = (total[:, :dh] / total[:, dh:]).astype(o_ref.dtype)


def fox_attention(proj, cum, *, batch, seq, blk):
    m = proj.shape[0]
    h, dh, nh = FOX_HEADS, HEAD_DIM, FOX_HEADS_PER_STEP
    assert dh == LANES and seq % blk == 0 and h % nh == 0
    nblk = seq // blk
    w = nh * dh
    kern = functools.partial(_fox_attn_kernel, blk=blk, nh=nh)
    return pl.pallas_call(
        kern, grid=(batch, h // nh, nblk),
        in_specs=[
            pl.BlockSpec((blk, w), lambda b, hh, i: (b * nblk + i, hh)),
            pl.BlockSpec((seq, w), lambda b, hh, i: (b, (h // nh) + hh)),
            pl.BlockSpec((seq, w), lambda b, hh, i: (b, 2 * (h // nh) + hh)),
            pl.BlockSpec((None, nh, 1, seq), lambda b, hh, i: (b, hh, 0, 0)),
        ],
        out_specs=pl.BlockSpec((blk, w), lambda b, hh, i: (b * nblk + i, hh)),
        out_shape=jax.ShapeDtypeStruct((m, h * dh), BF16),
        scratch_shapes=[pltpu.VMEM((nh, blk, LANES), F32), pltpu.VMEM((nh, blk, dh + LANES), F32),
                        pltpu.VMEM((nh, 1, 1), F32)],
        compiler_params=_params(("parallel", "parallel", "arbitrary"), flags=ATTN_FLAGS),
        name="fox_attention",
    )(proj, proj, proj, cum)


SB_LOGIT_CAP = 126.0
SB_DEAD_CARRY = SB_LOGIT_CAP + 150.0


SB_HEADS_PER_STEP = 4


def _sb_attn_kernel(q_ref, k_ref, v_ref, u_ref, o_ref, carry_scr, acc_scr, *, blk, sub, nh):
    qi = pl.program_id(2)
    q0 = pl.multiple_of(qi * blk, blk)
    nsub = blk // sub
    dh = HEAD_DIM

    def block(h, kstart, strict):
        hs = slice(h * dh, (h + 1) * dh)
        z = lax.dot_general(q_ref[:, hs], k_ref[pl.ds(kstart, blk), hs], (((1,), (1,)), ((), ())),
                            preferred_element_type=F32)
        z = jnp.minimum(z, SB_LOGIT_CAP)
        sp = jnp.log(1.0 + jnp.exp2(z)) * LOG2E
        if strict is not None:
            sp = jnp.where(strict, sp, 0.0)
        carry = carry_scr[h]
        parts = [None] * nsub
        for c in range(nsub - 1, -1, -1):
            sl = slice(c * sub, (c + 1) * sub)
            sp_c = sp[:, sl]
            g = jnp.dot(sp_c.astype(BF16), u_ref[...], preferred_element_type=F32)
            a_c = jnp.exp2(z[:, sl] - g - _rep(carry, sub // LANES))
            if strict is not None:
                a_c = jnp.where(strict[:, sl], a_c, 0.0)
            parts[c] = a_c.astype(BF16)
            carry = carry + jnp.sum(sp_c, axis=-1, keepdims=True)
        carry_scr[h] = carry
        acc_scr[:, hs] += jnp.dot(jnp.concatenate(parts, axis=1), v_ref[pl.ds(kstart, blk), hs],
                                  preferred_element_type=F32)

    carry_scr[...] = jnp.zeros_like(carry_scr)
    acc_scr[...] = jnp.zeros_like(acc_scr)
    row = lax.broadcasted_iota(jnp.int32, (blk, blk), 0)
    col = lax.broadcasted_iota(jnp.int32, (blk, blk), 1)
    for h in range(nh):
        block(h, q0, col < row)

    def alive():
        return jnp.min(carry_scr[...]) < SB_DEAD_CARRY

    def body(state):
        n, _ = state
        kstart = pl.multiple_of((qi - n) * blk, blk)
        for h in range(nh):
            block(h, kstart, None)
        return n + 1, alive()

    lax.while_loop(lambda st: jnp.logical_and(st[0] <= qi, st[1]), body, (jnp.int32(1), alive()))
    o_ref[...] = acc_scr[...].astype(o_ref.dtype)


def sb_attention(proj, *, batch, seq, blk, sub):
    m = proj.shape[0]
    h, dh, nh = SB_HEADS, HEAD_DIM, SB_HEADS_PER_STEP
    assert seq % blk == 0 and blk % sub == 0 and h % nh == 0
    nblk = seq // blk
    w = nh * dh
    u = jnp.asarray(np.tril(np.ones((sub, sub), np.float32)), BF16)
    kern = functools.partial(_sb_attn_kernel, blk=blk, sub=sub, nh=nh)
    return pl.pallas_call(
        kern, grid=(batch, h // nh, nblk),
        in_specs=[
            pl.BlockSpec((blk, w), lambda b, hh, i: (b * nblk + i, hh)),
            pl.BlockSpec((seq, w), lambda b, hh, i: (b, (h // nh) + hh)),
            pl.BlockSpec((seq, w), lambda b, hh, i: (b, 2 * (h // nh) + hh)),
            pl.BlockSpec((sub, sub), lambda b, hh, i: (0, 0)),
        ],
        out_specs=pl.BlockSpec((blk, w), lambda b, hh, i: (b * nblk + i, hh)),
        out_shape=jax.ShapeDtypeStruct((m, h * dh), BF16),
        scratch_shapes=[pltpu.VMEM((nh, blk, LANES), F32), pltpu.VMEM((blk, w), F32)],
        compiler_params=_params(("parallel", "parallel", "arbitrary")),
        name="sb_attention",
    )(proj, proj, proj, u)


LRU_TN = 3 * LANES
LRU_KW = 7 * LANES
LRU_HALO = 8


def _lru_windows(width, bd):
    starts = []
    for j in range(width // LRU_TN):
        c0, c1 = j * LRU_TN, (j + 1) * LRU_TN - 1
        r0, r1 = (c0 // bd) * bd, (c1 // bd + 1) * bd
        k0 = min((r0 // LANES) * LANES, width - LRU_KW)
        assert k0 <= r0 and r1 <= k0 + LRU_KW
        starts.append(k0)
    return starts


def _pack_lru_gates(w_r, w_i, width):
    nb, bd, _ = w_r.shape
    starts = _lru_windows(width, bd)
    dense_r = jax.scipy.linalg.block_diag(*w_r.astype(BF16))
    dense_i = jax.scipy.linalg.block_diag(*w_i.astype(BF16))
    tiles = []
    for j, k0 in enumerate(starts):
        cs = slice(j * LRU_TN, (j + 1) * LRU_TN)
        tiles.append(jnp.concatenate([dense_r[k0:k0 + LRU_KW, cs], dense_i[k0:k0 + LRU_KW, cs]], axis=1))
    return jnp.stack(tiles).astype(BF16), starts


def _lru_kernel(gate_ref, u_ref, cw_ref, cb_ref, wri_ref, br_ref, bi_ref, lam_ref, y_ref,
                ubuf, a_scr, b_scr, h_scr, *, starts, t):
    sblk = pl.program_id(1)
    width = u_ref.shape[1]

    @pl.when(sblk == 0)
    def _():
        ubuf[0:LRU_HALO, :] = jnp.zeros((LRU_HALO, width), F32)
        h_scr[...] = jnp.zeros_like(h_scr)

    ubuf[LRU_HALO:LRU_HALO + t, :] = u_ref[...]
    nconv = cw_ref.shape[0]
    uc = cb_ref[...] + cw_ref[nconv - 1:nconv, :] * u_ref[...]
    for j in range(nconv - 1):
        off = LRU_HALO - (nconv - 1) + j
        uc = uc + cw_ref[j:j + 1, :] * ubuf[off:off + t, :]
    ubuf[0:LRU_HALO, :] = u_ref[t - LRU_HALO:t, :]

    ucb = uc.astype(BF16)
    log_sig_lam = _log_sigmoid(lam_ref[...])
    row_in_group = lax.broadcasted_iota(jnp.int32, (t, LRU_TN), 0) % SUBLANES
    for j, k0 in enumerate(starts):
        cs = slice(j * LRU_TN, (j + 1) * LRU_TN)
        pre = jnp.dot(ucb[:, k0:k0 + LRU_KW], wri_ref[j], preferred_element_type=F32)
        r = jax.nn.sigmoid(pre[:, :LRU_TN] + br_ref[:, cs])
        ig = jax.nn.sigmoid(pre[:, LRU_TN:] + bi_ref[:, cs])
        log_a = LRU_C * r * log_sig_lam[:, cs]
        a = jnp.exp(log_a)
        bterm = jnp.sqrt(1.0 - a * a) * (ig * uc[:, cs])
        d = 1
        while d < SUBLANES:
            take = row_in_group >= d
            b_sh = pltpu.roll(bterm, d, 0)
            a_sh = pltpu.roll(a, d, 0)
            bterm = jnp.where(take, a * b_sh + bterm, bterm)
            a = jnp.where(take, a * a_sh, a)
            d *= 2
        a_scr[:, cs] = a
        b_scr[:, cs] = bterm

    h_in = h_scr[...]
    for r in range(t // SUBLANES):
        rows = slice(r * SUBLANES, (r + 1) * SUBLANES)
        h = b_scr[rows, :] + a_scr[rows, :] * h_in
        b_scr[rows, :] = h
        h_in = h[SUBLANES - 1:SUBLANES, :]
    h_scr[...] = h_in
    y_ref[...] = (b_scr[...] * jax.nn.gelu(gate_ref[...], approximate=True)).astype(y_ref.dtype)


def lru_core(proj, conv_w, conv_b, wri, starts, b_r, b_i, lam, *, batch, seq, t):
    m = proj.shape[0]
    width = proj.shape[1] // 2
    nt = seq // t
    assert seq % t == 0 and t % SUBLANES == 0
    vec = lambda: pl.BlockSpec((1, width), lambda b, s: (0, 0))
    kern = functools.partial(_lru_kernel, starts=tuple(starts), t=t)
    return pl.pallas_call(
        kern, grid=(batch, nt),
        in_specs=[pl.BlockSpec((t, width), lambda b, s: (b * nt + s, 0)),
                  pl.BlockSpec((t, width), lambda b, s: (b * nt + s, 1)),
                  pl.BlockSpec(conv_w.shape, lambda b, s: (0, 0)),
                  vec(),
                  pl.BlockSpec(wri.shape, lambda b, s: (0, 0, 0)),
                  vec(), vec(), vec()],
        out_specs=pl.BlockSpec((t, width), lambda b, s: (b * nt + s, 0)),
        out_shape=jax.ShapeDtypeStruct((m, width), BF16),
        scratch_shapes=[pltpu.VMEM((t + LRU_HALO, width), F32),
                        pltpu.VMEM((t, width), F32),
                        pltpu.VMEM((t, width), F32),
                        pltpu.VMEM((1, width), F32)],
        compiler_params=_params(("parallel", "arbitrary")),
        name="lru_core",
    )(proj, proj, conv_w, conv_b.reshape(1, width), wri, b_r.reshape(1, width),
      b_i.reshape(1, width), lam.reshape(1, width))


def _col_to_row(col, eye):
    return jnp.sum(jnp.where(eye, col, 0.0), axis=0, keepdims=True)


def _mlstm_kernel(q_ref, k_ref, v_ref, o_ref, g_ref, bif_ref, hg_ref, y_ref,
                  c_scr, n_scr, m_scr, *, nheads, dk, dv):
    chunk = pl.program_id(1)
    length = q_ref.shape[0]

    @pl.when(chunk == 0)
    def _():
        c_scr[...] = jnp.zeros_like(c_scr)
        n_scr[...] = jnp.zeros_like(n_scr)
        m_scr[...] = jnp.full_like(m_scr, MLSTM_M_INIT)

    row = lax.broadcasted_iota(jnp.int32, (length, length), 0)
    col = lax.broadcasted_iota(jnp.int32, (length, length), 1)
    eye = row == col
    causal = col <= row
    g = g_ref[...] + bif_ref[...]
    lane = lax.broadcasted_iota(jnp.int32, g.shape, 1)
    kscale = dk ** -0.5

    for head in range(nheads):
        ks_ = slice(head * dk, (head + 1) * dk)
        vs_ = slice(head * dv, (head + 1) * dv)
        i_col = jnp.sum(jnp.where(lane == head, g, 0.0), axis=-1, keepdims=True)
        f_col = jnp.sum(jnp.where(lane == nheads + head, g, 0.0), axis=-1, keepdims=True)
        lf_col = _log_sigmoid(f_col)
        lf_row = _col_to_row(lf_col, eye)
        i_row = _col_to_row(i_col, eye)
        b_col = jnp.sum(jnp.where(causal, lf_row, 0.0), axis=-1, keepdims=True)
        b_row = _col_to_row(b_col, eye)
        gsum = jnp.sum(lf_row, axis=-1, keepdims=True)

        m_prev = m_scr[head]
        dm = jnp.where(causal, b_col - b_row + i_row, NEG)
        inter = b_col + m_prev
        m_t = jnp.maximum(inter, jnp.max(dm, axis=-1, keepdims=True))
        w_intra = jnp.exp(dm - m_t)
        w_inter = jnp.exp(inter - m_t)

        q = q_ref[:, ks_]
        ks = k_ref[:, ks_] * kscale
        v = v_ref[:, vs_]
        s = lax.dot_general(q, ks, (((1,), (1,)), ((), ())), preferred_element_type=F32) * w_intra
        num = (w_inter * jnp.dot(q, c_scr[head].astype(BF16), preferred_element_type=F32)
               + jnp.dot(s.astype(BF16), v, preferred_element_type=F32))
        qn = jnp.sum(q.astype(F32) * n_scr[head], axis=-1, keepdims=True)
        den = w_inter * qn + jnp.sum(s, axis=-1, keepdims=True)
        hval = num / jnp.maximum(jnp.abs(den), jnp.exp(-m_t))

        hn = hval * _rms_scale(hval) * hg_ref[:, vs_]
        y_ref[:, vs_] = (jax.nn.sigmoid(o_ref[:, vs_].astype(F32)) * hn).astype(y_ref.dtype)

        wk_col = gsum - b_col + i_col
        m_new = jnp.maximum(gsum + m_prev, jnp.max(wk_col, axis=0, keepdims=True))
        decay = jnp.exp(gsum + m_prev - m_new)
        kw = jnp.exp(wk_col - m_new) * ks.astype(F32)
        c_scr[head] = decay * c_scr[head] + lax.dot_general(
            kw.astype(BF16), v, (((0,), (0,)), ((), ())), preferred_element_type=F32)
        n_scr[head] = decay * n_scr[head] + jnp.sum(kw, axis=0, keepdims=True)
        m_scr[head] = m_new


def mlstm_core(proj, gates, b_if, head_g, *, batch, seq, chunk):
    m = proj.shape[0]
    d = head_g.shape[0]
    nh = MLSTM_HEADS
    dk, dv = d // 2 // nh, d // nh
    qk = nh * dk
    nc = seq // chunk
    ng = gates.shape[1]
    bif = jnp.zeros((1, ng), F32).at[0, :2 * nh].set(b_if.reshape(-1))
    kern = functools.partial(_mlstm_kernel, nheads=nh, dk=dk, dv=dv)
    return pl.pallas_call(
        kern, grid=(batch, nc),
        in_specs=[pl.BlockSpec((chunk, qk), lambda b, c: (b * nc + c, 0)),
                  pl.BlockSpec((chunk, qk), lambda b, c: (b * nc + c, 1)),
                  pl.BlockSpec((chunk, d), lambda b, c: (b * nc + c, (2 * qk) // d)),
                  pl.BlockSpec((chunk, d), lambda b, c: (b * nc + c, (2 * qk) // d + 1)),
                  pl.BlockSpec((chunk, ng), lambda b, c: (b * nc + c, 0)),
                  pl.BlockSpec((1, ng), lambda b, c: (0, 0)),
                  pl.BlockSpec((1, d), lambda b, c: (0, 0))],
        out_specs=pl.BlockSpec((chunk, d), lambda b, c: (b * nc + c, 0)),
        out_shape=jax.ShapeDtypeStruct((m, d), BF16),
        scratch_shapes=[pltpu.VMEM((nh, dk, dv), F32), pltpu.VMEM((nh, 1, dk), F32),
                        pltpu.VMEM((nh, 1, 1), F32)],
        compiler_params=_params(("parallel", "arbitrary")),
        name="mlstm_core",
    )(proj, proj, proj, proj, gates, bif, head_g.reshape(1, d))


def _pad_cols(w, n):
    return jnp.pad(w, ((0, 0), (0, n - w.shape[1])))


def kernel(x, norm_g, mlp_w1, mlp_w2, fox_w_in, fox_b_f, fox_w_out, lru_w_in, lru_conv_w, lru_conv_b,
           lru_w_r, lru_b_r, lru_w_i, lru_b_i, lru_lambda, lru_w_out, sb_w_in, sb_w_out,
           mlstm_w_in, mlstm_b_if, mlstm_head_g, mlstm_w_out):
    batch, seq, d = x.shape
    depth = norm_g.shape[0]
    n_mixers = 4
    xf = x.reshape(batch * seq, d)
    tm_proj = 1024
    tm_out = 512
    attn_blk = min(512, seq)
    q_scale = HEAD_DIM ** -0.5 * LOG2E

    for i in range(depth):
        mixer, j = i % n_mixers, i // n_mixers
        if mixer == 0:
            w = fox_w_in[j]
            proj, fgate = norm_proj(xf, norm_g[i, 0], w.astype(BF16),
                                    _pad_cols(w[:, 3 * d:], LANES).astype(BF16), n_cols=3 * d,
                                    out_dtype=BF16, tm=tm_proj, tn=1024, q_scale=q_scale, q_cols=d)
            f_t = fgate[:, :FOX_HEADS].reshape(batch, seq, FOX_HEADS).transpose(0, 2, 1)
            cum = fox_cum(f_t, fox_b_f[j]).reshape(batch, FOX_HEADS, 1, seq)
            mixed = fox_attention(proj, cum, batch=batch, seq=seq, blk=attn_blk)
            w_out = fox_w_out[j]
        elif mixer == 1:
            width = lru_w_in.shape[2] // 2
            proj = norm_proj(xf, norm_g[i, 0], lru_w_in[j].astype(BF16), out_dtype=F32,
                             tm=tm_proj, tn=width // 3)
            wri, starts = _pack_lru_gates(lru_w_r[j], lru_w_i[j], width)
            mixed = lru_core(proj, lru_conv_w[j], lru_conv_b[j], wri, starts, lru_b_r[j], lru_b_i[j],
                             lru_lambda[j], batch=batch, seq=seq, t=min(256, seq))
            w_out = lru_w_out[j]
        elif mixer == 2:
            proj = norm_proj(xf, norm_g[i, 0], sb_w_in[j].astype(BF16), out_dtype=BF16,
                             tm=tm_proj, tn=1024, q_scale=q_scale, q_cols=d)
            mixed = sb_attention(proj, batch=batch, seq=seq, blk=attn_blk, sub=256)
            w_out = sb_w_out[j]
        else:
            w = mlstm_w_in[j]
            proj, gates = norm_proj(xf, norm_g[i, 0], w.astype(BF16),
                                    _pad_cols(w[:, 3 * d:], LANES).astype(BF16), n_cols=3 * d,
                                    out_dtype=BF16, tm=tm_proj, tn=1024)
            mixed = mlstm_core(proj, gates, mlstm_b_if[j], mlstm_head_g[j],
                               batch=batch, seq=seq, chunk=min(256, seq))
            w_out = mlstm_w_out[j]
        xf = proj_norm_res(mixed, w_out.astype(BF16), norm_g[i, 1], xf, tm=tm_out)
        xf = mlp_block(xf, norm_g[i, 2], mlp_w1[i].astype(BF16), mlp_w2[i].astype(BF16),
                       norm_g[i, 3], tm=512, tf=1024)
    return xf.reshape(batch, seq, d)
```

```python
import functools
import math

import numpy as np
import jax
import jax.numpy as jnp
from jax import lax
from jax.experimental import pallas as pl
from jax.experimental.pallas import tpu as pltpu

F32 = jnp.float32
BF16 = jnp.bfloat16

EPS = 1e-6
LANES = 128
SUBLANES = 8
VMEM_LIMIT = 56 * 1024 * 1024
NEG = -1e30
LOG2E = math.log2(math.e)

FOX_HEADS = 16
SB_HEADS = 16
HEAD_DIM = 128
LRU_BLOCKS = 16
LRU_C = 8.0
MLSTM_HEADS = 4
MLSTM_M_INIT = -1e30


def _params(sem, vmem=VMEM_LIMIT):
    return pltpu.CompilerParams(dimension_semantics=sem, vmem_limit_bytes=vmem)


def _log_sigmoid(x):
    return jnp.minimum(x, 0.0) - jnp.log1p(jnp.exp(-jnp.abs(x)))


def _rms_scale(x):
    return lax.rsqrt(jnp.mean(x * x, axis=-1, keepdims=True) + EPS)


def _project(h_scr, w_ref, o_ref, q_scale, q_tiles):
    acc = jnp.dot(h_scr[...], w_ref[...], preferred_element_type=F32)
    if q_tiles:
        acc = acc * jnp.where(pl.program_id(1) < q_tiles, q_scale, 1.0)
    o_ref[...] = acc.astype(o_ref.dtype)


def _norm_proj_kernel(x_ref, g_ref, w_ref, o_ref, h_scr, *, q_scale, q_tiles):
    @pl.when(pl.program_id(1) == 0)
    def _():
        x = x_ref[...]
        h_scr[...] = (x * _rms_scale(x) * g_ref[...]).astype(BF16)

    _project(h_scr, w_ref, o_ref, q_scale, q_tiles)


def _norm_proj_gate_kernel(x_ref, g_ref, w_ref, wg_ref, o_ref, og_ref, h_scr, *, q_scale, q_tiles):
    @pl.when(pl.program_id(1) == 0)
    def _():
        x = x_ref[...]
        h = (x * _rms_scale(x) * g_ref[...]).astype(BF16)
        h_scr[...] = h
        og_ref[...] = jnp.dot(h, wg_ref[...], preferred_element_type=F32)

    _project(h_scr, w_ref, o_ref, q_scale, q_tiles)


def norm_proj(x, g, w, wg=None, *, out_dtype, tm, tn, n_cols=None, q_scale=1.0, q_cols=0):
    m, d = x.shape
    n = w.shape[1] if n_cols is None else n_cols
    assert m % tm == 0 and n % tn == 0 and q_cols % tn == 0
    grid = (m // tm, n // tn)
    x_spec = pl.BlockSpec((tm, d), lambda i, j: (i, 0))
    g_spec = pl.BlockSpec((1, d), lambda i, j: (0, 0))
    w_spec = pl.BlockSpec((d, tn), lambda i, j: (0, j))
    o_spec = pl.BlockSpec((tm, tn), lambda i, j: (i, j))
    scratch = [pltpu.VMEM((tm, d), BF16)]
    statics = dict(q_scale=q_scale, q_tiles=q_cols // tn)
    if wg is None:
        return pl.pallas_call(
            functools.partial(_norm_proj_kernel, **statics), grid=grid,
            in_specs=[x_spec, g_spec, w_spec], out_specs=o_spec,
            out_shape=jax.ShapeDtypeStruct((m, n), out_dtype),
            scratch_shapes=scratch,
            compiler_params=_params(("parallel", "arbitrary")),
            name="norm_proj",
        )(x, g.reshape(1, d), w)
    ng = wg.shape[1]
    return pl.pallas_call(
        functools.partial(_norm_proj_gate_kernel, **statics), grid=grid,
        in_specs=[x_spec, g_spec, w_spec, pl.BlockSpec((d, ng), lambda i, j: (0, 0))],
        out_specs=[o_spec, pl.BlockSpec((tm, ng), lambda i, j: (i, 0))],
        out_shape=[jax.ShapeDtypeStruct((m, n), out_dtype), jax.ShapeDtypeStruct((m, ng), F32)],
        scratch_shapes=scratch,
        compiler_params=_params(("parallel", "arbitrary")),
        name="norm_proj_gate",
    )(x, g.reshape(1, d), w, wg)


def _proj_norm_res_kernel(a_ref, w_ref, g_ref, x_ref, o_ref):
    y = jnp.dot(a_ref[...], w_ref[...], preferred_element_type=F32)
    o_ref[...] = x_ref[...] + y * _rms_scale(y) * g_ref[...]


def proj_norm_res(a, w, g, x, *, tm):
    m, k = a.shape
    d = w.shape[1]
    assert m % tm == 0
    return pl.pallas_call(
        _proj_norm_res_kernel, grid=(m // tm,),
        in_specs=[pl.BlockSpec((tm, k), lambda i: (i, 0)),
                  pl.BlockSpec((k, d), lambda i: (0, 0)),
                  pl.BlockSpec((1, d), lambda i: (0, 0)),
                  pl.BlockSpec((tm, d), lambda i: (i, 0))],
        out_specs=pl.BlockSpec((tm, d), lambda i: (i, 0)),
        out_shape=jax.ShapeDtypeStruct((m, d), F32),
        compiler_params=_params(("parallel",)),
        name="proj_norm_res",
    )(a, w, g.reshape(1, d), x)


def _mlp_kernel(x_ref, g2_ref, w1_ref, w2_ref, g3_ref, o_ref, h_scr, acc_scr):
    f = pl.program_id(1)

    @pl.when(f == 0)
    def _():
        x = x_ref[...]
        h_scr[...] = (x * _rms_scale(x) * g2_ref[...]).astype(BF16)
        acc_scr[...] = jnp.zeros_like(acc_scr)

    a = jnp.dot(h_scr[...], w1_ref[...], preferred_element_type=F32)
    a = jnp.square(jnp.maximum(a, 0.0)).astype(BF16)
    acc_scr[...] += jnp.dot(a, w2_ref[...], preferred_element_type=F32)

    @pl.when(f == pl.num_programs(1) - 1)
    def _():
        y = acc_scr[...]
        o_ref[...] = x_ref[...] + y * _rms_scale(y) * g3_ref[...]


def mlp_block(x, g2, w1, w2, g3, *, tm, tf):
    m, d = x.shape
    ff = w1.shape[1]
    assert m % tm == 0 and ff % tf == 0
    return pl.pallas_call(
        _mlp_kernel, grid=(m // tm, ff // tf),
        in_specs=[pl.BlockSpec((tm, d), lambda i, f: (i, 0)),
                  pl.BlockSpec((1, d), lambda i, f: (0, 0)),
                  pl.BlockSpec((d, tf), lambda i, f: (0, f)),
                  pl.BlockSpec((tf, d), lambda i, f: (f, 0)),
                  pl.BlockSpec((1, d), lambda i, f: (0, 0))],
        out_specs=pl.BlockSpec((tm, d), lambda i, f: (i, 0)),
        out_shape=jax.ShapeDtypeStruct((m, d), F32),
        scratch_shapes=[pltpu.VMEM((tm, d), BF16), pltpu.VMEM((tm, d), F32)],
        compiler_params=_params(("parallel", "arbitrary")),
        name="mlp_block",
    )(x, g2.reshape(1, d), w1, w2, g3.reshape(1, d))


def _fox_cum_kernel(f_ref, b_ref, c_ref):
    v = _log_sigmoid(f_ref[...] + b_ref[...])
    pos = lax.broadcasted_iota(jnp.int32, v.shape, 1)
    d = 1
    while d < v.shape[1]:
        v = v + jnp.where(pos >= d, pltpu.roll(v, d, 1), 0.0)
        d *= 2
    c_ref[...] = v * LOG2E


def fox_cum(f_t, b_f):
    b, h, s = f_t.shape
    return pl.pallas_call(
        _fox_cum_kernel, grid=(b,),
        in_specs=[pl.BlockSpec((None, h, s), lambda i: (i, 0, 0)),
                  pl.BlockSpec((h, 1), lambda i: (0, 0))],
        out_specs=pl.BlockSpec((None, h, s), lambda i: (i, 0, 0)),
        out_shape=jax.ShapeDtypeStruct((b, h, s), F32),
        compiler_params=_params(("parallel",)),
        name="fox_cum",
    )(f_t, b_f.reshape(h, 1))


def _rep(x, n):
    return jnp.concatenate([x] * n, axis=1) if n > 1 else x


FOX_HEADS_PER_STEP = 4


FOX_DEAD_MARGIN = 152.0


def _fox_attn_kernel(q_ref, k_ref, v_ref, c_ref, o_ref, m_scr, acc_scr, knorm_scr, *, blk, nh):
    qi = pl.program_id(2)
    q0 = pl.multiple_of(qi * blk, blk)
    dh = HEAD_DIM
    ones = jnp.ones((blk, LANES), BF16)

    def block(h, kstart, causal):
        hs = slice(h * dh, (h + 1) * dh)
        s = lax.dot_general(q_ref[:, hs], k_ref[pl.ds(kstart, blk), hs], (((1,), (1,)), ((), ())),
                            preferred_element_type=F32)
        c0 = c_ref[h, :, pl.ds(q0, LANES)][:, 0:1]
        s = s + (c0 - c_ref[h, :, pl.ds(kstart, blk)])
        if causal is not None:
            s = jnp.where(causal, s, NEG)
        m_prev = m_scr[h]
        m_new = jnp.maximum(m_prev, jnp.max(s, axis=-1, keepdims=True))
        p = jnp.exp2(s - _rep(m_new, blk // LANES)).astype(BF16)
        alpha = jnp.exp2(m_prev - m_new)
        v = jnp.concatenate([v_ref[pl.ds(kstart, blk), hs], ones], axis=1)
        acc_scr[h] = _rep(alpha, 2) * acc_scr[h] + jnp.dot(p, v, preferred_element_type=F32)
        m_scr[h] = m_new

    @pl.when(qi == 0)
    def _():
        for h in range(nh):
            kk = k_ref[:, h * dh:(h + 1) * dh].astype(F32)
            knorm_scr[h] = jnp.sqrt(jnp.max(jnp.sum(kk * kk, axis=-1, keepdims=True), axis=0, keepdims=True))

    m_scr[...] = jnp.full_like(m_scr, NEG)
    acc_scr[...] = jnp.zeros_like(acc_scr)
    row = lax.broadcasted_iota(jnp.int32, (blk, blk), 0)
    col = lax.broadcasted_iota(jnp.int32, (blk, blk), 1)
    for h in range(nh):
        block(h, q0, col <= row)

    reach, c_start = [], []
    for h in range(nh):
        qq = q_ref[:, h * dh:(h + 1) * dh].astype(F32)
        qnorm = jnp.sqrt(jnp.max(jnp.sum(qq * qq, axis=-1, keepdims=True), axis=0, keepdims=True))
        reach.append(qnorm * knorm_scr[h])
        c_start.append(c_ref[h, :, pl.ds(q0, LANES)][:, 0:1])

    def block_start(n):
        return pl.multiple_of(jnp.maximum(qi - n, 0) * blk, blk)

    def alive(n):
        last_chunk = block_start(n) + (blk - LANES)
        live = False
        for h in range(nh):
            c_end = c_ref[h, :, pl.ds(last_chunk, LANES)][:, LANES - 1:LANES]
            bound = jnp.max(reach[h] + (c_start[h] - c_end))
            live = jnp.logical_or(live, bound - jnp.min(m_scr[h]) > -FOX_DEAD_MARGIN)
        return live

    def body(state):
        n, _ = state
        kstart = block_start(n)
        for h in range(nh):
            block(h, kstart, None)
        return n + 1, alive(n + 1)

    lax.while_loop(lambda st: jnp.logical_and(st[0] <= qi, st[1]), body, (jnp.int32(1), alive(1)))
    for h in range(nh):
        total = acc_scr[h]
        o_ref[:, h * dh:(h + 1) * dh] = (total[:, :dh] / total[:, dh:]).astype(o_ref.dtype)


def fox_attention(proj, cum, *, batch, seq, blk):
    m = proj.shape[0]
    h, dh, nh = FOX_HEADS, HEAD_DIM, FOX_HEADS_PER_STEP
    assert dh == LANES and seq % blk == 0 and h % nh == 0
    nblk = seq // blk
    w = nh * dh
    kern = functools.partial(_fox_attn_kernel, blk=blk, nh=nh)
    return pl.pallas_call(
        kern, grid=(batch, h // nh, nblk),
        in_specs=[
            pl.BlockSpec((blk, w), lambda b, hh, i: (b * nblk + i, hh)),
            pl.BlockSpec((seq, w), lambda b, hh, i: (b, (h // nh) + hh)),
            pl.BlockSpec((seq, w), lambda b, hh, i: (b, 2 * (h // nh) + hh)),
            pl.BlockSpec((None, nh, 1, seq), lambda b, hh, i: (b, hh, 0, 0)),
        ],
        out_specs=pl.BlockSpec((blk, w), lambda b, hh, i: (b * nblk + i, hh)),
        out_shape=jax.ShapeDtypeStruct((m, h * dh), BF16),
        scratch_shapes=[pltpu.VMEM((nh, blk, LANES), F32), pltpu.VMEM((nh, blk, dh + LANES), F32),
                        pltpu.VMEM((nh, 1, 1), F32)],
        compiler_params=_params(("parallel", "parallel", "arbitrary")),
        name="fox_attention",
    )(proj, proj, proj, cum)


SB_LOGIT_CAP = 126.0
SB_DEAD_CARRY = SB_LOGIT_CAP + 150.0


SB_HEADS_PER_STEP = 4


def _sb_attn_kernel(q_ref, k_ref, v_ref, u_ref, o_ref, carry_scr, acc_scr, *, blk, sub, nh):
    qi = pl.program_id(2)
    q0 = pl.multiple_of(qi * blk, blk)
    nsub = blk // sub
    dh = HEAD_DIM

    def block(h, kstart, strict):
        hs = slice(h * dh, (h + 1) * dh)
        z = lax.dot_general(q_ref[:, hs], k_ref[pl.ds(kstart, blk), hs], (((1,), (1,)), ((), ())),
                            preferred_element_type=F32)
        z = jnp.minimum(z, SB_LOGIT_CAP)
        sp = jnp.log(1.0 + jnp.exp2(z)) * LOG2E
        if strict is not None:
            sp = jnp.where(strict, sp, 0.0)
        carry = carry_scr[h]
        parts = [None] * nsub
        for c in range(nsub - 1, -1, -1):
            sl = slice(c * sub, (c + 1) * sub)
            sp_c = sp[:, sl]
            g = jnp.dot(sp_c.astype(BF16), u_ref[...], preferred_element_type=F32)
            a_c = jnp.exp2(z[:, sl] - g - _rep(carry, sub // LANES))
            if strict is not None:
                a_c = jnp.where(strict[:, sl], a_c, 0.0)
            parts[c] = a_c.astype(BF16)
            carry = carry + jnp.sum(sp_c, axis=-1, keepdims=True)
        carry_scr[h] = carry
        acc_scr[:, hs] += jnp.dot(jnp.concatenate(parts, axis=1), v_ref[pl.ds(kstart, blk), hs],
                                  preferred_element_type=F32)

    carry_scr[...] = jnp.zeros_like(carry_scr)
    acc_scr[...] = jnp.zeros_like(acc_scr)
    row = lax.broadcasted_iota(jnp.int32, (blk, blk), 0)
    col = lax.broadcasted_iota(jnp.int32, (blk, blk), 1)
    for h in range(nh):
        block(h, q0, col < row)

    def alive():
        return jnp.min(carry_scr[...]) < SB_DEAD_CARRY

    def body(state):
        n, _ = state
        kstart = pl.multiple_of((qi - n) * blk, blk)
        for h in range(nh):
            block(h, kstart, None)
        return n + 1, alive()

    lax.while_loop(lambda st: jnp.logical_and(st[0] <= qi, st[1]), body, (jnp.int32(1), alive()))
    o_ref[...] = acc_scr[...].astype(o_ref.dtype)


def sb_attention(proj, *, batch, seq, blk, sub):
    m = proj.shape[0]
    h, dh, nh = SB_HEADS, HEAD_DIM, SB_HEADS_PER_STEP
    assert seq % blk == 0 and blk % sub == 0 and h % nh == 0
    nblk = seq // blk
    w = nh * dh
    u = jnp.asarray(np.tril(np.ones((sub, sub), np.float32)), BF16)
    kern = functools.partial(_sb_attn_kernel, blk=blk, sub=sub, nh=nh)
    return pl.pallas_call(
        kern, grid=(batch, h // nh, nblk),
        in_specs=[
            pl.BlockSpec((blk, w), lambda b, hh, i: (b * nblk + i, hh)),
            pl.BlockSpec((seq, w), lambda b, hh, i: (b, (h // nh) + hh)),
            pl.BlockSpec((seq, w), lambda b, hh, i: (b, 2 * (h // nh) + hh)),
            pl.BlockSpec((sub, sub), lambda b, hh, i: (0, 0)),
        ],
        out_specs=pl.BlockSpec((blk, w), lambda b, hh, i: (b * nblk + i, hh)),
        out_shape=jax.ShapeDtypeStruct((m, h * dh), BF16),
        scratch_shapes=[pltpu.VMEM((nh, blk, LANES), F32), pltpu.VMEM((blk, w), F32)],
        compiler_params=_params(("parallel", "parallel", "arbitrary")),
        name="sb_attention",
    )(proj, proj, proj, u)


LRU_TN = 3 * LANES
LRU_KW = 7 * LANES
LRU_HALO = 8


def _lru_windows(width, bd):
    starts = []
    for j in range(width // LRU_TN):
        c0, c1 = j * LRU_TN, (j + 1) * LRU_TN - 1
        r0, r1 = (c0 // bd) * bd, (c1 // bd + 1) * bd
        k0 = min((r0 // LANES) * LANES, width - LRU_KW)
        assert k0 <= r0 and r1 <= k0 + LRU_KW
        starts.append(k0)
    return starts


def _pack_lru_gates(w_r, w_i, width):
    nb, bd, _ = w_r.shape
    starts = _lru_windows(width, bd)
    dense_r = jax.scipy.linalg.block_diag(*w_r.astype(BF16))
    dense_i = jax.scipy.linalg.block_diag(*w_i.astype(BF16))
    tiles = []
    for j, k0 in enumerate(starts):
        cs = slice(j * LRU_TN, (j + 1) * LRU_TN)
        tiles.append(jnp.concatenate([dense_r[k0:k0 + LRU_KW, cs], dense_i[k0:k0 + LRU_KW, cs]], axis=1))
    return jnp.stack(tiles).astype(BF16), starts


def _lru_kernel(gate_ref, u_ref, cw_ref, cb_ref, wri_ref, br_ref, bi_ref, lam_ref, y_ref,
                ubuf, a_scr, b_scr, h_scr, *, starts, t):
    sblk = pl.program_id(1)
    width = u_ref.shape[1]

    @pl.when(sblk == 0)
    def _():
        ubuf[0:LRU_HALO, :] = jnp.zeros((LRU_HALO, width), F32)
        h_scr[...] = jnp.zeros_like(h_scr)

    ubuf[LRU_HALO:LRU_HALO + t, :] = u_ref[...]
    nconv = cw_ref.shape[0]
    uc = cb_ref[...] + cw_ref[nconv - 1:nconv, :] * u_ref[...]
    for j in range(nconv - 1):
        off = LRU_HALO - (nconv - 1) + j
        uc = uc + cw_ref[j:j + 1, :] * ubuf[off:off + t, :]
    ubuf[0:LRU_HALO, :] = u_ref[t - LRU_HALO:t, :]

    ucb = uc.astype(BF16)
    log_sig_lam = _log_sigmoid(lam_ref[...])
    row_in_group = lax.broadcasted_iota(jnp.int32, (1, SUBLANES, LRU_TN), 1)
    for j, k0 in enumerate(starts):
        cs = slice(j * LRU_TN, (j + 1) * LRU_TN)
        pre = jnp.dot(ucb[:, k0:k0 + LRU_KW], wri_ref[j], preferred_element_type=F32)
        r = jax.nn.sigmoid(pre[:, :LRU_TN] + br_ref[:, cs])
        ig = jax.nn.sigmoid(pre[:, LRU_TN:] + bi_ref[:, cs])
        log_a = LRU_C * r * log_sig_lam[:, cs]
        a = jnp.exp(log_a)
        bterm = jnp.sqrt(1.0 - a * a) * (ig * uc[:, cs])
        a = a.reshape(t // SUBLANES, SUBLANES, LRU_TN)
        bterm = bterm.reshape(t // SUBLANES, SUBLANES, LRU_TN)
        d = 1
        while d < SUBLANES:
            take = row_in_group >= d
            b_sh = pltpu.roll(bterm, d, 1)
            a_sh = pltpu.roll(a, d, 1)
            bterm = jnp.where(take, a * b_sh + bterm, bterm)
            a = jnp.where(take, a * a_sh, a)
            d *= 2
        a_scr[:, cs] = a.reshape(t, LRU_TN)
        b_scr[:, cs] = bterm.reshape(t, LRU_TN)

    h_in = h_scr[...]
    for r in range(t // SUBLANES):
        rows = slice(r * SUBLANES, (r + 1) * SUBLANES)
        h = b_scr[rows, :] + a_scr[rows, :] * h_in
        b_scr[rows, :] = h
        h_in = h[SUBLANES - 1:SUBLANES, :]
    h_scr[...] = h_in
    y_ref[...] = (b_scr[...] * jax.nn.gelu(gate_ref[...], approximate=True)).astype(y_ref.dtype)


def lru_core(proj, conv_w, conv_b, wri, starts, b_r, b_i, lam, *, batch, seq, t):
    m = proj.shape[0]
    width = proj.shape[1] // 2
    nt = seq // t
    assert seq % t == 0 and t % SUBLANES == 0
    vec = lambda: pl.BlockSpec((1, width), lambda b, s: (0, 0))
    kern = functools.partial(_lru_kernel, starts=tuple(starts), t=t)
    return pl.pallas_call(
        kern, grid=(batch, nt),
        in_specs=[pl.BlockSpec((t, width), lambda b, s: (b * nt + s, 0)),
                  pl.BlockSpec((t, width), lambda b, s: (b * nt + s, 1)),
                  pl.BlockSpec(conv_w.shape, lambda b, s: (0, 0)),
                  vec(),
                  pl.BlockSpec(wri.shape, lambda b, s: (0, 0, 0)),
                  vec(), vec(), vec()],
        out_specs=pl.BlockSpec((t, width), lambda b, s: (b * nt + s, 0)),
        out_shape=jax.ShapeDtypeStruct((m, width), BF16),
        scratch_shapes=[pltpu.VMEM((t + LRU_HALO, width), F32),
                        pltpu.VMEM((t, width), F32),
                        pltpu.VMEM((t, width), F32),
                        pltpu.VMEM((1, width), F32)],
        compiler_params=_params(("parallel", "arbitrary")),
        name="lru_core",
    )(proj, proj, conv_w, conv_b.reshape(1, width), wri, b_r.reshape(1, width),
      b_i.reshape(1, width), lam.reshape(1, width))


def _col_to_row(col, eye):
    return jnp.sum(jnp.where(eye, col, 0.0), axis=0, keepdims=True)


def _mlstm_kernel(q_ref, k_ref, v_ref, o_ref, g_ref, bif_ref, hg_ref, y_ref,
                  c_scr, n_scr, m_scr, *, nheads, dk, dv):
    chunk = pl.program_id(1)
    length = q_ref.shape[0]

    @pl.when(chunk == 0)
    def _():
        c_scr[...] = jnp.zeros_like(c_scr)
        n_scr[...] = jnp.zeros_like(n_scr)
        m_scr[...] = jnp.full_like(m_scr, MLSTM_M_INIT)

    row = lax.broadcasted_iota(jnp.int32, (length, length), 0)
    col = lax.broadcasted_iota(jnp.int32, (length, length), 1)
    eye = row == col
    causal = col <= row
    g = g_ref[...] + bif_ref[...]
    lane = lax.broadcasted_iota(jnp.int32, g.shape, 1)
    kscale = dk ** -0.5

    for head in range(nheads):
        ks_ = slice(head * dk, (head + 1) * dk)
        vs_ = slice(head * dv, (head + 1) * dv)
        i_col = jnp.sum(jnp.where(lane == head, g, 0.0), axis=-1, keepdims=True)
        f_col = jnp.sum(jnp.where(lane == nheads + head, g, 0.0), axis=-1, keepdims=True)
        lf_col = _log_sigmoid(f_col)
        lf_row = _col_to_row(lf_col, eye)
        i_row = _col_to_row(i_col, eye)
        b_col = jnp.sum(jnp.where(causal, lf_row, 0.0), axis=-1, keepdims=True)
        b_row = _col_to_row(b_col, eye)
        gsum = jnp.sum(lf_row, axis=-1, keepdims=True)

        m_prev = m_scr[head]
        dm = jnp.where(causal, b_col - b_row + i_row, NEG)
        inter = b_col + m_prev
        m_t = jnp.maximum(inter, jnp.max(dm, axis=-1, keepdims=True))
        w_intra = jnp.exp(dm - m_t)
        w_inter = jnp.exp(inter - m_t)

        q = q_ref[:, ks_]
        ks = k_ref[:, ks_] * kscale
        v = v_ref[:, vs_]
        s = lax.dot_general(q, ks, (((1,), (1,)), ((), ())), preferred_element_type=F32) * w_intra
        num = (w_inter * jnp.dot(q, c_scr[head].astype(BF16), preferred_element_type=F32)
               + jnp.dot(s.astype(BF16), v, preferred_element_type=F32))
        qn = jnp.sum(q.astype(F32) * n_scr[head], axis=-1, keepdims=True)
        den = w_inter * qn + jnp.sum(s, axis=-1, keepdims=True)
        hval = num / jnp.maximum(jnp.abs(den), jnp.exp(-m_t))

        hn = hval * _rms_scale(hval) * hg_ref[:, vs_]
        y_ref[:, vs_] = (jax.nn.sigmoid(o_ref[:, vs_].astype(F32)) * hn).astype(y_ref.dtype)

        wk_col = gsum - b_col + i_col
        m_new = jnp.maximum(gsum + m_prev, jnp.max(wk_col, axis=0, keepdims=True))
        decay = jnp.exp(gsum + m_prev - m_new)
        kw = jnp.exp(wk_col - m_new) * ks.astype(F32)
        c_scr[head] = decay * c_scr[head] + lax.dot_general(
            kw.astype(BF16), v, (((0,), (0,)), ((), ())), preferred_element_type=F32)
        n_scr[head] = decay * n_scr[head] + jnp.sum(kw, axis=0, keepdims=True)
        m_scr[head] = m_new


def mlstm_core(proj, gates, b_if, head_g, *, batch, seq, chunk):
    m = proj.shape[0]
    d = head_g.shape[0]
    nh = MLSTM_HEADS
    dk, dv = d // 2 // nh, d // nh
    qk = nh * dk
    nc = seq // chunk
    ng = gates.shape[1]
    bif = jnp.zeros((1, ng), F32).at[0, :2 * nh].set(b_if.reshape(-1))
    kern = functools.partial(_mlstm_kernel, nheads=nh, dk=dk, dv=dv)
    return pl.pallas_call(
        kern, grid=(batch, nc),
        in_specs=[pl.BlockSpec((chunk, qk), lambda b, c: (b * nc + c, 0)),
                  pl.BlockSpec((chunk, qk), lambda b, c: (b * nc + c, 1)),
                  pl.BlockSpec((chunk, d), lambda b, c: (b * nc + c, (2 * qk) // d)),
                  pl.BlockSpec((chunk, d), lambda b, c: (b * nc + c, (2 * qk) // d + 1)),
                  pl.BlockSpec((chunk, ng), lambda b, c: (b * nc + c, 0)),
                  pl.BlockSpec((1, ng), lambda b, c: (0, 0)),
                  pl.BlockSpec((1, d), lambda b, c: (0, 0))],
        out_specs=pl.BlockSpec((chunk, d), lambda b, c: (b * nc + c, 0)),
        out_shape=jax.ShapeDtypeStruct((m, d), BF16),
        scratch_shapes=[pltpu.VMEM((nh, dk, dv), F32), pltpu.VMEM((nh, 1, dk), F32),
                        pltpu.VMEM((nh, 1, 1), F32)],
        compiler_params=_params(("parallel", "arbitrary")),
        name="mlstm_core",
    )(proj, proj, proj, proj, gates, bif, head_g.reshape(1, d))


def _pad_cols(w, n):
    return jnp.pad(w, ((0, 0), (0, n - w.shape[1])))


def _tiles(seq):
    return dict(
        proj_rows=1024,
        proj_cols=1024,
        out_rows=512,
        mlp_rows=512,
        mlp_cols=1024,
        attn_blk=min(512, seq),
        sb_sub=256,
        lru_rows=min(256, seq),
        mlstm_chunk=min(256, seq),
    )


def kernel(x, norm_g, mlp_w1, mlp_w2, fox_w_in, fox_b_f, fox_w_out, lru_w_in, lru_conv_w, lru_conv_b,
           lru_w_r, lru_b_r, lru_w_i, lru_b_i, lru_lambda, lru_w_out, sb_w_in, sb_w_out,
           mlstm_w_in, mlstm_b_if, mlstm_head_g, mlstm_w_out):
    batch, seq, d = x.shape
    depth = norm_g.shape[0]
    n_mixers = 4
    tl = _tiles(seq)
    xf = x.reshape(batch * seq, d)
    q_scale = HEAD_DIM ** -0.5 * LOG2E
    proj_tiles = dict(tm=tl["proj_rows"], tn=tl["proj_cols"])

    for i in range(depth):
        mixer, j = i % n_mixers, i // n_mixers
        if mixer == 0:
            w = fox_w_in[j]
            proj, fgate = norm_proj(xf, norm_g[i, 0], w.astype(BF16),
                                    _pad_cols(w[:, 3 * d:], LANES).astype(BF16), n_cols=3 * d,
                                    out_dtype=BF16, q_scale=q_scale, q_cols=d, **proj_tiles)
            f_t = fgate[:, :FOX_HEADS].reshape(batch, seq, FOX_HEADS).transpose(0, 2, 1)
            cum = fox_cum(f_t, fox_b_f[j]).reshape(batch, FOX_HEADS, 1, seq)
            mixed = fox_attention(proj, cum, batch=batch, seq=seq, blk=tl["attn_blk"])
            w_out = fox_w_out[j]
        elif mixer == 1:
            width = lru_w_in.shape[2] // 2
            proj = norm_proj(xf, norm_g[i, 0], lru_w_in[j].astype(BF16), out_dtype=F32,
                             tm=tl["proj_rows"], tn=width // 3)
            wri, starts = _pack_lru_gates(lru_w_r[j], lru_w_i[j], width)
            mixed = lru_core(proj, lru_conv_w[j], lru_conv_b[j], wri, starts, lru_b_r[j], lru_b_i[j],
                             lru_lambda[j], batch=batch, seq=seq, t=tl["lru_rows"])
            w_out = lru_w_out[j]
        elif mixer == 2:
            proj = norm_proj(xf, norm_g[i, 0], sb_w_in[j].astype(BF16), out_dtype=BF16,
                             q_scale=q_scale, q_cols=d, **proj_tiles)
            mixed = sb_attention(proj, batch=batch, seq=seq, blk=tl["attn_blk"], sub=tl["sb_sub"])
            w_out = sb_w_out[j]
        else:
            w = mlstm_w_in[j]
            proj, gates = norm_proj(xf, norm_g[i, 0], w.astype(BF16),
                                    _pad_cols(w[:, 3 * d:], LANES).astype(BF16), n_cols=3 * d,
                                    out_dtype=BF16, **proj_tiles)
            mixed = mlstm_core(proj, gates, mlstm_b_if[j], mlstm_head_g[j],
                               batch=batch, seq=seq, chunk=tl["mlstm_chunk"])
            w_out = mlstm_w_out[j]
        xf = proj_norm_res(mixed, w_out.astype(BF16), norm_g[i, 1], xf, tm=tl["out_rows"])
        xf = mlp_block(xf, norm_g[i, 2], mlp_w1[i].astype(BF16), mlp_w2[i].astype(BF16),
                       norm_g[i, 3], tm=tl["mlp_rows"], tf=tl["mlp_cols"])
    return xf.reshape(batch, seq, d)
```

```python
import functools
import math

import numpy as np
import jax
import jax.numpy as jnp
from jax import lax
from jax.experimental import pallas as pl
from jax.experimental.pallas import tpu as pltpu

F32 = jnp.float32
BF16 = jnp.bfloat16

EPS = 1e-6
LANES = 128
SUBLANES = 8
VMEM_LIMIT = 56 * 1024 * 1024
NEG = -1e30
LOG2E = math.log2(math.e)

FOX_HEADS = 16
SB_HEADS = 16
HEAD_DIM = 128
LRU_BLOCKS = 16
LRU_C = 8.0
MLSTM_HEADS = 4
MLSTM_M_INIT = -1e30


def _params(sem, vmem=VMEM_LIMIT):
    return pltpu.CompilerParams(dimension_semantics=sem, vmem_limit_bytes=vmem)


def _log_sigmoid(x):
    return jnp.minimum(x, 0.0) - jnp.log1p(jnp.exp(-jnp.abs(x)))


def _rms_scale(x):
    return lax.rsqrt(jnp.mean(x * x, axis=-1, keepdims=True) + EPS)


def _project(h_scr, w_ref, o_ref, q_scale, q_tiles):
    acc = jnp.dot(h_scr[...], w_ref[...], preferred_element_type=F32)
    if q_tiles:
        acc = acc * jnp.where(pl.program_id(1) < q_tiles, q_scale, 1.0)
    o_ref[...] = acc.astype(o_ref.dtype)


def _norm_proj_kernel(x_ref, g_ref, w_ref, o_ref, h_scr, *, q_scale, q_tiles):
    @pl.when(pl.program_id(1) == 0)
    def _():
        x = x_ref[...]
        h_scr[...] = (x * _rms_scale(x) * g_ref[...]).astype(BF16)

    _project(h_scr, w_ref, o_ref, q_scale, q_tiles)


def _norm_proj_gate_kernel(x_ref, g_ref, w_ref, wg_ref, o_ref, og_ref, h_scr, *, q_scale, q_tiles):
    @pl.when(pl.program_id(1) == 0)
    def _():
        x = x_ref[...]
        h = (x * _rms_scale(x) * g_ref[...]).astype(BF16)
        h_scr[...] = h
        og_ref[...] = jnp.dot(h, wg_ref[...], preferred_element_type=F32)

    _project(h_scr, w_ref, o_ref, q_scale, q_tiles)


def norm_proj(x, g, w, wg=None, *, out_dtype, tm, tn, n_cols=None, q_scale=1.0, q_cols=0):
    m, d = x.shape
    n = w.shape[1] if n_cols is None else n_cols
    assert m % tm == 0 and n % tn == 0 and q_cols % tn == 0
    grid = (m // tm, n // tn)
    x_spec = pl.BlockSpec((tm, d), lambda i, j: (i, 0))
    g_spec = pl.BlockSpec((1, d), lambda i, j: (0, 0))
    w_spec = pl.BlockSpec((d, tn), lambda i, j: (0, j))
    o_spec = pl.BlockSpec((tm, tn), lambda i, j: (i, j))
    scratch = [pltpu.VMEM((tm, d), BF16)]
    statics = dict(q_scale=q_scale, q_tiles=q_cols // tn)
    if wg is None:
        return pl.pallas_call(
            functools.partial(_norm_proj_kernel, **statics), grid=grid,
            in_specs=[x_spec, g_spec, w_spec], out_specs=o_spec,
            out_shape=jax.ShapeDtypeStruct((m, n), out_dtype),
            scratch_shapes=scratch,
            compiler_params=_params(("parallel", "arbitrary")),
            name="norm_proj",
        )(x, g.reshape(1, d), w)
    ng = wg.shape[1]
    return pl.pallas_call(
        functools.partial(_norm_proj_gate_kernel, **statics), grid=grid,
        in_specs=[x_spec, g_spec, w_spec, pl.BlockSpec((d, ng), lambda i, j: (0, 0))],
        out_specs=[o_spec, pl.BlockSpec((tm, ng), lambda i, j: (i, 0))],
        out_shape=[jax.ShapeDtypeStruct((m, n), out_dtype), jax.ShapeDtypeStruct((m, ng), F32)],
        scratch_shapes=scratch,
        compiler_params=_params(("parallel", "arbitrary")),
        name="norm_proj_gate",
    )(x, g.reshape(1, d), w, wg)


def _proj_norm_res_kernel(a_ref, w_ref, g_ref, x_ref, o_ref):
    y = jnp.dot(a_ref[...], w_ref[...], preferred_element_type=F32)
    o_ref[...] = x_ref[...] + y * _rms_scale(y) * g_ref[...]


def proj_norm_res(a, w, g, x, *, tm):
    m, k = a.shape
    d = w.shape[1]
    assert m % tm == 0
    return pl.pallas_call(
        _proj_norm_res_kernel, grid=(m // tm,),
        in_specs=[pl.BlockSpec((tm, k), lambda i: (i, 0)),
                  pl.BlockSpec((k, d), lambda i: (0, 0)),
                  pl.BlockSpec((1, d), lambda i: (0, 0)),
                  pl.BlockSpec((tm, d), lambda i: (i, 0))],
        out_specs=pl.BlockSpec((tm, d), lambda i: (i, 0)),
        out_shape=jax.ShapeDtypeStruct((m, d), F32),
        compiler_params=_params(("parallel",)),
        name="proj_norm_res",
    )(a, w, g.reshape(1, d), x)


def _mlp_kernel(x_ref, g2_ref, w1_ref, w2_ref, g3_ref, o_ref, h_scr, acc_scr):
    f = pl.program_id(1)

    @pl.when(f == 0)
    def _():
        x = x_ref[...]
        h_scr[...] = (x * _rms_scale(x) * g2_ref[...]).astype(BF16)
        acc_scr[...] = jnp.zeros_like(acc_scr)

    a = jnp.dot(h_scr[...], w1_ref[...], preferred_element_type=F32)
    a = jnp.square(jnp.maximum(a, 0.0)).astype(BF16)
    acc_scr[...] += jnp.dot(a, w2_ref[...], preferred_element_type=F32)

    @pl.when(f == pl.num_programs(1) - 1)
    def _():
        y = acc_scr[...]
        o_ref[...] = x_ref[...] + y * _rms_scale(y) * g3_ref[...]


def mlp_block(x, g2, w1, w2, g3, *, layer, tm, tf):
    m, d = x.shape
    ff = w1.shape[2]
    assert m % tm == 0 and ff % tf == 0
    return pl.pallas_call(
        _mlp_kernel, grid=(m // tm, ff // tf),
        in_specs=[pl.BlockSpec((tm, d), lambda i, f: (i, 0)),
                  pl.BlockSpec((1, d), lambda i, f: (0, 0)),
                  pl.BlockSpec((None, d, tf), lambda i, f: (layer, 0, f)),
                  pl.BlockSpec((None, tf, d), lambda i, f: (layer, f, 0)),
                  pl.BlockSpec((1, d), lambda i, f: (0, 0))],
        out_specs=pl.BlockSpec((tm, d), lambda i, f: (i, 0)),
        out_shape=jax.ShapeDtypeStruct((m, d), F32),
        scratch_shapes=[pltpu.VMEM((tm, d), BF16), pltpu.VMEM((tm, d), F32)],
        compiler_params=_params(("parallel", "arbitrary")),
        name="mlp_block",
    )(x, g2.reshape(1, d), w1, w2, g3.reshape(1, d))


def _fox_cum_kernel(f_ref, b_ref, c_ref):
    v = _log_sigmoid(f_ref[...] + b_ref[...])
    pos = lax.broadcasted_iota(jnp.int32, v.shape, 1)
    d = 1
    while d < v.shape[1]:
        v = v + jnp.where(pos >= d, pltpu.roll(v, d, 1), 0.0)
        d *= 2
    c_ref[...] = v * LOG2E


def fox_cum(f_t, b_f):
    b, h, s = f_t.shape
    return pl.pallas_call(
        _fox_cum_kernel, grid=(b,),
        in_specs=[pl.BlockSpec((None, h, s), lambda i: (i, 0, 0)),
                  pl.BlockSpec((h, 1), lambda i: (0, 0))],
        out_specs=pl.BlockSpec((None, h, s), lambda i: (i, 0, 0)),
        out_shape=jax.ShapeDtypeStruct((b, h, s), F32),
        compiler_params=_params(("parallel",)),
        name="fox_cum",
    )(f_t, b_f.reshape(h, 1))


def _rep(x, n):
    return jnp.concatenate([x] * n, axis=1) if n > 1 else x


FOX_HEADS_PER_STEP = 4


FOX_DEAD_MARGIN = 152.0


def _fox_attn_kernel(q_ref, k_ref, v_ref, c_ref, o_ref, m_scr, acc_scr, knorm_scr, *, blk, nh):
    qi = pl.program_id(2)
    q0 = pl.multiple_of(qi * blk, blk)
    dh = HEAD_DIM
    ones = jnp.ones((blk, LANES), BF16)

    def block(h, kstart, causal):
        hs = slice(h * dh, (h + 1) * dh)
        s = lax.dot_general(q_ref[:, hs], k_ref[pl.ds(kstart, blk), hs], (((1,), (1,)), ((), ())),
                            preferred_element_type=F32)
        c0 = c_ref[h, :, pl.ds(q0, LANES)][:, 0:1]
        s = s + (c0 - c_ref[h, :, pl.ds(kstart, blk)])
        if causal is not None:
            s = jnp.where(causal, s, NEG)
        m_prev = m_scr[h]
        m_new = jnp.maximum(m_prev, jnp.max(s, axis=-1, keepdims=True))
        p = jnp.exp2(s - _rep(m_new, blk // LANES)).astype(BF16)
        alpha = jnp.exp2(m_prev - m_new)
        v = jnp.concatenate([v_ref[pl.ds(kstart, blk), hs], ones], axis=1)
        acc_scr[h] = _rep(alpha, 2) * acc_scr[h] + jnp.dot(p, v, preferred_element_type=F32)
        m_scr[h] = m_new

    @pl.when(qi == 0)
    def _():
        for h in range(nh):
            kk = k_ref[:, h * dh:(h + 1) * dh].astype(F32)
            knorm_scr[h] = jnp.sqrt(jnp.max(jnp.sum(kk * kk, axis=-1, keepdims=True), axis=0, keepdims=True))

    m_scr[...] = jnp.full_like(m_scr, NEG)
    acc_scr[...] = jnp.zeros_like(acc_scr)
    row = lax.broadcasted_iota(jnp.int32, (blk, blk), 0)
    col = lax.broadcasted_iota(jnp.int32, (blk, blk), 1)
    for h in range(nh):
        block(h, q0, col <= row)

    reach, c_start = [], []
    for h in range(nh):
        qq = q_ref[:, h * dh:(h + 1) * dh].astype(F32)
        qnorm = jnp.sqrt(jnp.max(jnp.sum(qq * qq, axis=-1, keepdims=True), axis=0, keepdims=True))
        reach.append(qnorm * knorm_scr[h])
        c_start.append(c_ref[h, :, pl.ds(q0, LANES)][:, 0:1])

    def block_start(n):
        return pl.multiple_of(jnp.maximum(qi - n, 0) * blk, blk)

    def alive(n):
        last_chunk = block_start(n) + (blk - LANES)
        live = False
        for h in range(nh):
            c_end = c_ref[h, :, pl.ds(last_chunk, LANES)][:, LANES - 1:LANES]
            bound = jnp.max(reach[h] + (c_start[h] - c_end))
            live = jnp.logical_or(live, bound - jnp.min(m_scr[h]) > -FOX_DEAD_MARGIN)
        return live

    def body(state):
        n, _ = state
        kstart = block_start(n)
        for h in range(nh):
            block(h, kstart, None)
        return n + 1, alive(n + 1)

    lax.while_loop(lambda st: jnp.logical_and(st[0] <= qi, st[1]), body, (jnp.int32(1), alive(1)))
    for h in range(nh):
        total = acc_scr[h]
        o_ref[:, h * dh:(h + 1) * dh] = (total[:, :dh] / total[:, dh:]).astype(o_ref.dtype)


def fox_attention(proj, cum, *, batch, seq, blk):
    m = proj.shape[0]
    h, dh, nh = FOX_HEADS, HEAD_DIM, FOX_HEADS_PER_STEP
    assert dh == LANES and seq % blk == 0 and h % nh == 0
    nblk = seq // blk
    w = nh * dh
    kern = functools.partial(_fox_attn_kernel, blk=blk, nh=nh)
    return pl.pallas_call(
        kern, grid=(batch, h // nh, nblk),
        in_specs=[
            pl.BlockSpec((blk, w), lambda b, hh, i: (b * nblk + i, hh)),
            pl.BlockSpec((seq, w), lambda b, hh, i: (b, (h // nh) + hh)),
            pl.BlockSpec((seq, w), lambda b, hh, i: (b, 2 * (h // nh) + hh)),
            pl.BlockSpec((None, nh, 1, seq), lambda b, hh, i: (b, hh, 0, 0)),
        ],
        out_specs=pl.BlockSpec((blk, w), lambda b, hh, i: (b * nblk + i, hh)),
        out_shape=jax.ShapeDtypeStruct((m, h * dh), BF16),
        scratch_shapes=[pltpu.VMEM((nh, blk, LANES), F32), pltpu.VMEM((nh, blk, dh + LANES), F32),
                        pltpu.VMEM((nh, 1, 1), F32)],
        compiler_params=_params(("parallel", "parallel", "arbitrary")),
        name="fox_attention",
    )(proj, proj, proj, cum)


SB_LOGIT_CAP = 126.0
SB_DEAD_CARRY = SB_LOGIT_CAP + 150.0


SB_HEADS_PER_STEP = 4


def _sb_attn_kernel(q_ref, k_ref, v_ref, u_ref, o_ref, carry_scr, acc_scr, *, blk, sub, nh):
    qi = pl.program_id(2)
    q0 = pl.multiple_of(qi * blk, blk)
    nsub = blk // sub
    dh = HEAD_DIM

    def block(h, kstart, strict):
        hs = slice(h * dh, (h + 1) * dh)
        z = lax.dot_general(q_ref[:, hs], k_ref[pl.ds(kstart, blk), hs], (((1,), (1,)), ((), ())),
                            preferred_element_type=F32)
        z = jnp.minimum(z, SB_LOGIT_CAP)
        sp = jnp.log(1.0 + jnp.exp2(z)) * LOG2E
        if strict is not None:
            sp = jnp.where(strict, sp, 0.0)
        carry = carry_scr[h]
        parts = [None] * nsub
        for c in range(nsub - 1, -1, -1):
            sl = slice(c * sub, (c + 1) * sub)
            sp_c = sp[:, sl]
            g = jnp.dot(sp_c.astype(BF16), u_ref[...], preferred_element_type=F32)
            a_c = jnp.exp2(z[:, sl] - g - _rep(carry, sub // LANES))
            if strict is not None:
                a_c = jnp.where(strict[:, sl], a_c, 0.0)
            parts[c] = a_c.astype(BF16)
            carry = carry + jnp.sum(sp_c, axis=-1, keepdims=True)
        carry_scr[h] = carry
        acc_scr[:, hs] += jnp.dot(jnp.concatenate(parts, axis=1), v_ref[pl.ds(kstart, blk), hs],
                                  preferred_element_type=F32)

    carry_scr[...] = jnp.zeros_like(carry_scr)
    acc_scr[...] = jnp.zeros_like(acc_scr)
    row = lax.broadcasted_iota(jnp.int32, (blk, blk), 0)
    col = lax.broadcasted_iota(jnp.int32, (blk, blk), 1)
    for h in range(nh):
        block(h, q0, col < row)

    def alive():
        return jnp.min(carry_scr[...]) < SB_DEAD_CARRY

    def body(state):
        n, _ = state
        kstart = pl.multiple_of((qi - n) * blk, blk)
        for h in range(nh):
            block(h, kstart, None)
        return n + 1, alive()

    lax.while_loop(lambda st: jnp.logical_and(st[0] <= qi, st[1]), body, (jnp.int32(1), alive()))
    o_ref[...] = acc_scr[...].astype(o_ref.dtype)


def sb_attention(proj, *, batch, seq, blk, sub):
    m = proj.shape[0]
    h, dh, nh = SB_HEADS, HEAD_DIM, SB_HEADS_PER_STEP
    assert seq % blk == 0 and blk % sub == 0 and h % nh == 0
    nblk = seq // blk
    w = nh * dh
    u = jnp.asarray(np.tril(np.ones((sub, sub), np.float32)), BF16)
    kern = functools.partial(_sb_attn_kernel, blk=blk, sub=sub, nh=nh)
    return pl.pallas_call(
        kern, grid=(batch, h // nh, nblk),
        in_specs=[
            pl.BlockSpec((blk, w), lambda b, hh, i: (b * nblk + i, hh)),
            pl.BlockSpec((seq, w), lambda b, hh, i: (b, (h // nh) + hh)),
            pl.BlockSpec((seq, w), lambda b, hh, i: (b, 2 * (h // nh) + hh)),
            pl.BlockSpec((sub, sub), lambda b, hh, i: (0, 0)),
        ],
        out_specs=pl.BlockSpec((blk, w), lambda b, hh, i: (b * nblk + i, hh)),
        out_shape=jax.ShapeDtypeStruct((m, h * dh), BF16),
        scratch_shapes=[pltpu.VMEM((nh, blk, LANES), F32), pltpu.VMEM((blk, w), F32)],
        compiler_params=_params(("parallel", "parallel", "arbitrary")),
        name="sb_attention",
    )(proj, proj, proj, u)


LRU_TN = 3 * LANES
LRU_KW = 7 * LANES
LRU_HALO = 8


def _lru_windows(width, bd):
    starts = []
    for j in range(width // LRU_TN):
        c0, c1 = j * LRU_TN, (j + 1) * LRU_TN - 1
        r0, r1 = (c0 // bd) * bd, (c1 // bd + 1) * bd
        k0 = min((r0 // LANES) * LANES, width - LRU_KW)
        assert k0 <= r0 and r1 <= k0 + LRU_KW
        starts.append(k0)
    return starts


def _pack_lru_gates(w_r, w_i, width):
    nb, bd, _ = w_r.shape
    starts = _lru_windows(width, bd)
    dense_r = jax.scipy.linalg.block_diag(*w_r.astype(BF16))
    dense_i = jax.scipy.linalg.block_diag(*w_i.astype(BF16))
    tiles = []
    for j, k0 in enumerate(starts):
        cs = slice(j * LRU_TN, (j + 1) * LRU_TN)
        tiles.append(jnp.concatenate([dense_r[k0:k0 + LRU_KW, cs], dense_i[k0:k0 + LRU_KW, cs]], axis=1))
    return jnp.stack(tiles).astype(BF16), starts


def _lru_kernel(gate_ref, u_ref, cw_ref, cb_ref, wri_ref, br_ref, bi_ref, lam_ref, y_ref,
                ubuf, a_scr, b_scr, h_scr, *, starts, t):
    sblk = pl.program_id(1)
    width = u_ref.shape[1]

    @pl.when(sblk == 0)
    def _():
        ubuf[0:LRU_HALO, :] = jnp.zeros((LRU_HALO, width), F32)
        h_scr[...] = jnp.zeros_like(h_scr)

    ubuf[LRU_HALO:LRU_HALO + t, :] = u_ref[...]
    nconv = cw_ref.shape[0]
    uc = cb_ref[...] + cw_ref[nconv - 1:nconv, :] * u_ref[...]
    for j in range(nconv - 1):
        off = LRU_HALO - (nconv - 1) + j
        uc = uc + cw_ref[j:j + 1, :] * ubuf[off:off + t, :]
    ubuf[0:LRU_HALO, :] = u_ref[t - LRU_HALO:t, :]

    ucb = uc.astype(BF16)
    log_sig_lam = _log_sigmoid(lam_ref[...])
    row_in_group = lax.broadcasted_iota(jnp.int32, (1, SUBLANES, LRU_TN), 1)
    for j, k0 in enumerate(starts):
        cs = slice(j * LRU_TN, (j + 1) * LRU_TN)
        pre = jnp.dot(ucb[:, k0:k0 + LRU_KW], wri_ref[j], preferred_element_type=F32)
        r = jax.nn.sigmoid(pre[:, :LRU_TN] + br_ref[:, cs])
        ig = jax.nn.sigmoid(pre[:, LRU_TN:] + bi_ref[:, cs])
        log_a = LRU_C * r * log_sig_lam[:, cs]
        a = jnp.exp(log_a)
        bterm = jnp.sqrt(1.0 - a * a) * (ig * uc[:, cs])
        a = a.reshape(t // SUBLANES, SUBLANES, LRU_TN)
        bterm = bterm.reshape(t // SUBLANES, SUBLANES, LRU_TN)
        d = 1
        while d < SUBLANES:
            take = row_in_group >= d
            b_sh = pltpu.roll(bterm, d, 1)
            a_sh = pltpu.roll(a, d, 1)
            bterm = jnp.where(take, a * b_sh + bterm, bterm)
            a = jnp.where(take, a * a_sh, a)
            d *= 2
        a_scr[:, cs] = a.reshape(t, LRU_TN)
        b_scr[:, cs] = bterm.reshape(t, LRU_TN)

    h_in = h_scr[...]
    for r in range(t // SUBLANES):
        rows = slice(r * SUBLANES, (r + 1) * SUBLANES)
        h = b_scr[rows, :] + a_scr[rows, :] * h_in
        b_scr[rows, :] = h
        h_in = h[SUBLANES - 1:SUBLANES, :]
    h_scr[...] = h_in
    y_ref[...] = (b_scr[...] * jax.nn.gelu(gate_ref[...], approximate=True)).astype(y_ref.dtype)


def lru_core(proj, conv_w, conv_b, wri, starts, b_r, b_i, lam, *, batch, seq, t):
    m = proj.shape[0]
    width = proj.shape[1] // 2
    nt = seq // t
    assert seq % t == 0 and t % SUBLANES == 0
    vec = lambda: pl.BlockSpec((1, width), lambda b, s: (0, 0))
    kern = functools.partial(_lru_kernel, starts=tuple(starts), t=t)
    return pl.pallas_call(
        kern, grid=(batch, nt),
        in_specs=[pl.BlockSpec((t, width), lambda b, s: (b * nt + s, 0)),
                  pl.BlockSpec((t, width), lambda b, s: (b * nt + s, 1)),
                  pl.BlockSpec(conv_w.shape, lambda b, s: (0, 0)),
                  vec(),
                  pl.BlockSpec(wri.shape, lambda b, s: (0, 0, 0)),
                  vec(), vec(), vec()],
        out_specs=pl.BlockSpec((t, width), lambda b, s: (b * nt + s, 0)),
        out_shape=jax.ShapeDtypeStruct((m, width), BF16),
        scratch_shapes=[pltpu.VMEM((t + LRU_HALO, width), F32),
                        pltpu.VMEM((t, width), F32),
                        pltpu.VMEM((t, width), F32),
                        pltpu.VMEM((1, width), F32)],
        compiler_params=_params(("parallel", "arbitrary")),
        name="lru_core",
    )(proj, proj, conv_w, conv_b.reshape(1, width), wri, b_r.reshape(1, width),
      b_i.reshape(1, width), lam.reshape(1, width))


def _col_to_row(col, eye):
    return jnp.sum(jnp.where(eye, col, 0.0), axis=0, keepdims=True)


def _mlstm_kernel(q_ref, k_ref, v_ref, o_ref, g_ref, bif_ref, hg_ref, y_ref,
                  c_scr, n_scr, m_scr, *, nheads, dk, dv):
    chunk = pl.program_id(1)
    length = q_ref.shape[0]

    @pl.when(chunk == 0)
    def _():
        c_scr[...] = jnp.zeros_like(c_scr)
        n_scr[...] = jnp.zeros_like(n_scr)
        m_scr[...] = jnp.full_like(m_scr, MLSTM_M_INIT)

    row = lax.broadcasted_iota(jnp.int32, (length, length), 0)
    col = lax.broadcasted_iota(jnp.int32, (length, length), 1)
    eye = row == col
    causal = col <= row
    g = g_ref[...] + bif_ref[...]
    lane = lax.broadcasted_iota(jnp.int32, g.shape, 1)
    kscale = dk ** -0.5

    for head in range(nheads):
        ks_ = slice(head * dk, (head + 1) * dk)
        vs_ = slice(head * dv, (head + 1) * dv)
        i_col = jnp.sum(jnp.where(lane == head, g, 0.0), axis=-1, keepdims=True)
        f_col = jnp.sum(jnp.where(lane == nheads + head, g, 0.0), axis=-1, keepdims=True)
        lf_col = _log_sigmoid(f_col)
        lf_row = _col_to_row(lf_col, eye)
        i_row = _col_to_row(i_col, eye)
        b_col = jnp.sum(jnp.where(causal, lf_row, 0.0), axis=-1, keepdims=True)
        b_row = _col_to_row(b_col, eye)
        gsum = jnp.sum(lf_row, axis=-1, keepdims=True)

        m_prev = m_scr[head]
        dm = jnp.where(causal, b_col - b_row + i_row, NEG)
        inter = b_col + m_prev
        m_t = jnp.maximum(inter, jnp.max(dm, axis=-1, keepdims=True))
        w_intra = jnp.exp(dm - m_t)
        w_inter = jnp.exp(inter - m_t)

        q = q_ref[:, ks_]
        ks = k_ref[:, ks_] * kscale
        v = v_ref[:, vs_]
        s = lax.dot_general(q, ks, (((1,), (1,)), ((), ())), preferred_element_type=F32) * w_intra
        num = (w_inter * jnp.dot(q, c_scr[head].astype(BF16), preferred_element_type=F32)
               + jnp.dot(s.astype(BF16), v, preferred_element_type=F32))
        qn = jnp.sum(q.astype(F32) * n_scr[head], axis=-1, keepdims=True)
        den = w_inter * qn + jnp.sum(s, axis=-1, keepdims=True)
        hval = num / jnp.maximum(jnp.abs(den), jnp.exp(-m_t))

        hn = hval * _rms_scale(hval) * hg_ref[:, vs_]
        y_ref[:, vs_] = (jax.nn.sigmoid(o_ref[:, vs_].astype(F32)) * hn).astype(y_ref.dtype)

        wk_col = gsum - b_col + i_col
        m_new = jnp.maximum(gsum + m_prev, jnp.max(wk_col, axis=0, keepdims=True))
        decay = jnp.exp(gsum + m_prev - m_new)
        kw = jnp.exp(wk_col - m_new) * ks.astype(F32)
        c_scr[head] = decay * c_scr[head] + lax.dot_general(
            kw.astype(BF16), v, (((0,), (0,)), ((), ())), preferred_element_type=F32)
        n_scr[head] = decay * n_scr[head] + jnp.sum(kw, axis=0, keepdims=True)
        m_scr[head] = m_new


def mlstm_core(proj, gates, b_if, head_g, *, batch, seq, chunk):
    m = proj.shape[0]
    d = head_g.shape[0]
    nh = MLSTM_HEADS
    dk, dv = d // 2 // nh, d // nh
    qk = nh * dk
    nc = seq // chunk
    ng = gates.shape[1]
    bif = jnp.zeros((1, ng), F32).at[0, :2 * nh].set(b_if.reshape(-1))
    kern = functools.partial(_mlstm_kernel, nheads=nh, dk=dk, dv=dv)
    return pl.pallas_call(
        kern, grid=(batch, nc),
        in_specs=[pl.BlockSpec((chunk, qk), lambda b, c: (b * nc + c, 0)),
                  pl.BlockSpec((chunk, qk), lambda b, c: (b * nc + c, 1)),
                  pl.BlockSpec((chunk, d), lambda b, c: (b * nc + c, (2 * qk) // d)),
                  pl.BlockSpec((chunk, d), lambda b, c: (b * nc + c, (2 * qk) // d + 1)),
                  pl.BlockSpec((chunk, ng), lambda b, c: (b * nc + c, 0)),
                  pl.BlockSpec((1, ng), lambda b, c: (0, 0)),
                  pl.BlockSpec((1, d), lambda b, c: (0, 0))],
        out_specs=pl.BlockSpec((chunk, d), lambda b, c: (b * nc + c, 0)),
        out_shape=jax.ShapeDtypeStruct((m, d), BF16),
        scratch_shapes=[pltpu.VMEM((nh, dk, dv), F32), pltpu.VMEM((nh, 1, dk), F32),
                        pltpu.VMEM((nh, 1, 1), F32)],
        compiler_params=_params(("parallel", "arbitrary")),
        name="mlstm_core",
    )(proj, proj, proj, proj, gates, bif, head_g.reshape(1, d))


def _pad_cols(w, n):
    return jnp.pad(w, ((0, 0), (0, n - w.shape[1])))


def _tiles(seq):
    return dict(
        proj_rows=1024,
        proj_cols=2048,
        out_rows=512,
        mlp_rows=512,
        mlp_cols=1024,
        attn_blk=min(512, seq),
        sb_sub=256,
        lru_rows=min(256, seq),
        mlstm_chunk=min(256, seq),
    )


def kernel(x, norm_g, mlp_w1, mlp_w2, fox_w_in, fox_b_f, fox_w_out, lru_w_in, lru_conv_w, lru_conv_b,
           lru_w_r, lru_b_r, lru_w_i, lru_b_i, lru_lambda, lru_w_out, sb_w_in, sb_w_out,
           mlstm_w_in, mlstm_b_if, mlstm_head_g, mlstm_w_out):
    batch, seq, d = x.shape
    depth = norm_g.shape[0]
    n_mixers = 4
    tl = _tiles(seq)
    xf = x.reshape(batch * seq, d)
    q_scale = HEAD_DIM ** -0.5 * LOG2E
    proj_tiles = dict(tm=tl["proj_rows"], tn=tl["proj_cols"])
    w1_all, w2_all = mlp_w1.astype(BF16), mlp_w2.astype(BF16)

    for i in range(depth):
        mixer, j = i % n_mixers, i // n_mixers
        if mixer == 0:
            w = fox_w_in[j]
            proj, fgate = norm_proj(xf, norm_g[i, 0], w.astype(BF16),
                                    _pad_cols(w[:, 3 * d:], LANES).astype(BF16), n_cols=3 * d,
                                    out_dtype=BF16, q_scale=q_scale, q_cols=d, **proj_tiles)
            f_t = fgate[:, :FOX_HEADS].reshape(batch, seq, FOX_HEADS).transpose(0, 2, 1)
            cum = fox_cum(f_t, fox_b_f[j]).reshape(batch, FOX_HEADS, 1, seq)
            mixed = fox_attention(proj, cum, batch=batch, seq=seq, blk=tl["attn_blk"])
            w_out = fox_w_out[j]
        elif mixer == 1:
            width = lru_w_in.shape[2] // 2
            proj = norm_proj(xf, norm_g[i, 0], lru_w_in[j].astype(BF16), out_dtype=F32,
                             tm=tl["proj_rows"], tn=width // 3)
            wri, starts = _pack_lru_gates(lru_w_r[j], lru_w_i[j], width)
            mixed = lru_core(proj, lru_conv_w[j], lru_conv_b[j], wri, starts, lru_b_r[j], lru_b_i[j],
                             lru_lambda[j], batch=batch, seq=seq, t=tl["lru_rows"])
            w_out = lru_w_out[j]
        elif mixer == 2:
            proj = norm_proj(xf, norm_g[i, 0], sb_w_in[j].astype(BF16), out_dtype=BF16,
                             q_scale=q_scale, q_cols=d, **proj_tiles)
            mixed = sb_attention(proj, batch=batch, seq=seq, blk=tl["attn_blk"], sub=tl["sb_sub"])
            w_out = sb_w_out[j]
        else:
            w = mlstm_w_in[j]
            proj, gates = norm_proj(xf, norm_g[i, 0], w.astype(BF16),
                                    _pad_cols(w[:, 3 * d:], LANES).astype(BF16), n_cols=3 * d,
                                    out_dtype=BF16, **proj_tiles)
            mixed = mlstm_core(proj, gates, mlstm_b_if[j], mlstm_head_g[j],
                               batch=batch, seq=seq, chunk=tl["mlstm_chunk"])
            w_out = mlstm_w_out[j]
        xf = proj_norm_res(mixed, w_out.astype(BF16), norm_g[i, 1], xf, tm=tl["out_rows"])
        xf = mlp_block(xf, norm_g[i, 2], w1_all, w2_all, norm_g[i, 3], layer=i,
                       tm=tl["mlp_rows"], tf=tl["mlp_cols"])
    return xf.reshape(batch, seq, d)
```

```python
import functools
import math

import numpy as np
import jax
import jax.numpy as jnp
from jax import lax
from jax.experimental import pallas as pl
from jax.experimental.pallas import tpu as pltpu

F32 = jnp.float32
BF16 = jnp.bfloat16

EPS = 1e-6
LANES = 128
SUBLANES = 8
VMEM_LIMIT = 56 * 1024 * 1024
NEG = -1e30
LOG2E = math.log2(math.e)

FOX_HEADS = 16
SB_HEADS = 16
HEAD_DIM = 128
LRU_BLOCKS = 16
LRU_C = 8.0
MLSTM_HEADS = 4
MLSTM_M_INIT = -1e30


def _params(sem, vmem=VMEM_LIMIT):
    return pltpu.CompilerParams(dimension_semantics=sem, vmem_limit_bytes=vmem)


def _log_sigmoid(x):
    return jnp.minimum(x, 0.0) - jnp.log1p(jnp.exp(-jnp.abs(x)))


def _rms_scale(x):
    return lax.rsqrt(jnp.mean(x * x, axis=-1, keepdims=True) + EPS)


def _project(h_scr, w_ref, o_ref, q_scale, q_tiles):
    acc = jnp.dot(h_scr[...], w_ref[...], preferred_element_type=F32)
    if q_tiles:
        acc = acc * jnp.where(pl.program_id(1) < q_tiles, q_scale, 1.0)
    o_ref[...] = acc.astype(o_ref.dtype)


def _norm_proj_kernel(x_ref, g_ref, w_ref, o_ref, h_scr, *, q_scale, q_tiles):
    @pl.when(pl.program_id(1) == 0)
    def _():
        x = x_ref[...]
        h_scr[...] = (x * _rms_scale(x) * g_ref[...]).astype(BF16)

    _project(h_scr, w_ref, o_ref, q_scale, q_tiles)


def _norm_proj_gate_kernel(x_ref, g_ref, w_ref, wg_ref, o_ref, og_ref, h_scr, *, q_scale, q_tiles):
    @pl.when(pl.program_id(1) == 0)
    def _():
        x = x_ref[...]
        h = (x * _rms_scale(x) * g_ref[...]).astype(BF16)
        h_scr[...] = h
        og_ref[...] = jnp.dot(h, wg_ref[...], preferred_element_type=F32)

    _project(h_scr, w_ref, o_ref, q_scale, q_tiles)


def norm_proj(x, g, w, wg=None, *, out_dtype, tm, tn, n_cols=None, q_scale=1.0, q_cols=0):
    m, d = x.shape
    n = w.shape[1] if n_cols is None else n_cols
    assert m % tm == 0 and n % tn == 0 and q_cols % tn == 0
    grid = (m // tm, n // tn)
    x_spec = pl.BlockSpec((tm, d), lambda i, j: (i, 0))
    g_spec = pl.BlockSpec((1, d), lambda i, j: (0, 0))
    w_spec = pl.BlockSpec((d, tn), lambda i, j: (0, j))
    o_spec = pl.BlockSpec((tm, tn), lambda i, j: (i, j))
    scratch = [pltpu.VMEM((tm, d), BF16)]
    statics = dict(q_scale=q_scale, q_tiles=q_cols // tn)
    if wg is None:
        return pl.pallas_call(
            functools.partial(_norm_proj_kernel, **statics), grid=grid,
            in_specs=[x_spec, g_spec, w_spec], out_specs=o_spec,
            out_shape=jax.ShapeDtypeStruct((m, n), out_dtype),
            scratch_shapes=scratch,
            compiler_params=_params(("parallel", "arbitrary")),
            name="norm_proj",
        )(x, g.reshape(1, d), w)
    ng = wg.shape[1]
    return pl.pallas_call(
        functools.partial(_norm_proj_gate_kernel, **statics), grid=grid,
        in_specs=[x_spec, g_spec, w_spec, pl.BlockSpec((d, ng), lambda i, j: (0, 0))],
        out_specs=[o_spec, pl.BlockSpec((tm, ng), lambda i, j: (i, 0))],
        out_shape=[jax.ShapeDtypeStruct((m, n), out_dtype), jax.ShapeDtypeStruct((m, ng), F32)],
        scratch_shapes=scratch,
        compiler_params=_params(("parallel", "arbitrary")),
        name="norm_proj_gate",
    )(x, g.reshape(1, d), w, wg)


def _proj_norm_res_kernel(a_ref, w_ref, g_ref, x_ref, o_ref):
    y = jnp.dot(a_ref[...], w_ref[...], preferred_element_type=F32)
    o_ref[...] = x_ref[...] + y * _rms_scale(y) * g_ref[...]


def proj_norm_res(a, w, g, x, *, tm):
    m, k = a.shape
    d = w.shape[1]
    assert m % tm == 0
    return pl.pallas_call(
        _proj_norm_res_kernel, grid=(m // tm,),
        in_specs=[pl.BlockSpec((tm, k), lambda i: (i, 0)),
                  pl.BlockSpec((k, d), lambda i: (0, 0)),
                  pl.BlockSpec((1, d), lambda i: (0, 0)),
                  pl.BlockSpec((tm, d), lambda i: (i, 0))],
        out_specs=pl.BlockSpec((tm, d), lambda i: (i, 0)),
        out_shape=jax.ShapeDtypeStruct((m, d), F32),
        compiler_params=_params(("parallel",)),
        name="proj_norm_res",
    )(a, w, g.reshape(1, d), x)


def _mlp_kernel(x_ref, g2_ref, w1_ref, w2_ref, g3_ref, o_ref, h_scr, acc_scr):
    f = pl.program_id(1)

    @pl.when(f == 0)
    def _():
        x = x_ref[...]
        h_scr[...] = (x * _rms_scale(x) * g2_ref[...]).astype(BF16)
        acc_scr[...] = jnp.zeros_like(acc_scr)

    a = jnp.dot(h_scr[...], w1_ref[...], preferred_element_type=F32)
    a = jnp.square(jnp.maximum(a, 0.0)).astype(BF16)
    acc_scr[...] += jnp.dot(a, w2_ref[...], preferred_element_type=F32)

    @pl.when(f == pl.num_programs(1) - 1)
    def _():
        y = acc_scr[...]
        o_ref[...] = x_ref[...] + y * _rms_scale(y) * g3_ref[...]


def mlp_block(x, g2, w1, w2, g3, *, layer, tm, tf):
    m, d = x.shape
    ff = w1.shape[2]
    assert m % tm == 0 and ff % tf == 0
    return pl.pallas_call(
        _mlp_kernel, grid=(m // tm, ff // tf),
        in_specs=[pl.BlockSpec((tm, d), lambda i, f: (i, 0)),
                  pl.BlockSpec((1, d), lambda i, f: (0, 0)),
                  pl.BlockSpec((None, d, tf), lambda i, f: (layer, 0, f)),
                  pl.BlockSpec((None, tf, d), lambda i, f: (layer, f, 0)),
                  pl.BlockSpec((1, d), lambda i, f: (0, 0))],
        out_specs=pl.BlockSpec((tm, d), lambda i, f: (i, 0)),
        out_shape=jax.ShapeDtypeStruct((m, d), F32),
        scratch_shapes=[pltpu.VMEM((tm, d), BF16), pltpu.VMEM((tm, d), F32)],
        compiler_params=_params(("parallel", "arbitrary")),
        name="mlp_block",
    )(x, g2.reshape(1, d), w1, w2, g3.reshape(1, d))


def _fox_cum_kernel(f_ref, b_ref, c_ref):
    v = _log_sigmoid(f_ref[...] + b_ref[...])
    pos = lax.broadcasted_iota(jnp.int32, v.shape, 1)
    d = 1
    while d < v.shape[1]:
        v = v + jnp.where(pos >= d, pltpu.roll(v, d, 1), 0.0)
        d *= 2
    c_ref[...] = v * LOG2E


def fox_cum(f_t, b_f):
    b, h, s = f_t.shape
    return pl.pallas_call(
        _fox_cum_kernel, grid=(b,),
        in_specs=[pl.BlockSpec((None, h, s), lambda i: (i, 0, 0)),
                  pl.BlockSpec((h, 1), lambda i: (0, 0))],
        out_specs=pl.BlockSpec((None, h, s), lambda i: (i, 0, 0)),
        out_shape=jax.ShapeDtypeStruct((b, h, s), F32),
        compiler_params=_params(("parallel",)),
        name="fox_cum",
    )(f_t, b_f.reshape(h, 1))


def _rep(x, n):
    return jnp.concatenate([x] * n, axis=1) if n > 1 else x


FOX_HEADS_PER_STEP = 4


FOX_DEAD_MARGIN = 152.0


def _fox_attn_kernel(q_ref, k_ref, v_ref, c_ref, o_ref, m_scr, acc_scr, knorm_scr, *, blk, nh):
    qi = pl.program_id(2)
    q0 = pl.multiple_of(qi * blk, blk)
    dh = HEAD_DIM
    ones = jnp.ones((blk, LANES), BF16)

    def block(h, kstart, causal):
        hs = slice(h * dh, (h + 1) * dh)
        s = lax.dot_general(q_ref[:, hs], k_ref[pl.ds(kstart, blk), hs], (((1,), (1,)), ((), ())),
                            preferred_element_type=F32)
        c0 = c_ref[h, :, pl.ds(q0, LANES)][:, 0:1]
        s = s + (c0 - c_ref[h, :, pl.ds(kstart, blk)])
        if causal is not None:
            s = jnp.where(causal, s, NEG)
        m_prev = m_scr[h]
        m_new = jnp.maximum(m_prev, jnp.max(s, axis=-1, keepdims=True))
        p = jnp.exp2(s - _rep(m_new, blk // LANES)).astype(BF16)
        alpha = jnp.exp2(m_prev - m_new)
        v = jnp.concatenate([v_ref[pl.ds(kstart, blk), hs], ones], axis=1)
        acc_scr[h] = _rep(alpha, 2) * acc_scr[h] + jnp.dot(p, v, preferred_element_type=F32)
        m_scr[h] = m_new

    @pl.when(qi == 0)
    def _():
        for h in range(nh):
            kk = k_ref[:, h * dh:(h + 1) * dh].astype(F32)
            knorm_scr[h] = jnp.sqrt(jnp.max(jnp.sum(kk * kk, axis=-1, keepdims=True), axis=0, keepdims=True))

    m_scr[...] = jnp.full_like(m_scr, NEG)
    acc_scr[...] = jnp.zeros_like(acc_scr)
    row = lax.broadcasted_iota(jnp.int32, (blk, blk), 0)
    col = lax.broadcasted_iota(jnp.int32, (blk, blk), 1)
    for h in range(nh):
        block(h, q0, col <= row)

    reach, c_start = [], []
    for h in range(nh):
        qq = q_ref[:, h * dh:(h + 1) * dh].astype(F32)
        qnorm = jnp.sqrt(jnp.max(jnp.sum(qq * qq, axis=-1, keepdims=True), axis=0, keepdims=True))
        reach.append(qnorm * knorm_scr[h])
        c_start.append(c_ref[h, :, pl.ds(q0, LANES)][:, 0:1])

    def block_start(n):
        return pl.multiple_of(jnp.maximum(qi - n, 0) * blk, blk)

    def alive(n):
        last_chunk = block_start(n) + (blk - LANES)
        live = False
        for h in range(nh):
            c_end = c_ref[h, :, pl.ds(last_chunk, LANES)][:, LANES - 1:LANES]
            bound = jnp.max(reach[h] + (c_start[h] - c_end))
            live = jnp.logical_or(live, bound - jnp.min(m_scr[h]) > -FOX_DEAD_MARGIN)
        return live

    def body(state):
        n, _ = state
        kstart = block_start(n)
        for h in range(nh):
            block(h, kstart, None)
        return n + 1, alive(n + 1)

    lax.while_loop(lambda st: jnp.logical_and(st[0] <= qi, st[1]), body, (jnp.int32(1), alive(1)))
    for h in range(nh):
        total = acc_scr[h]
        o_ref[:, h * dh:(h + 1) * dh] = (total[:, :dh] / total[:, dh:]).astype(o_ref.dtype)


def fox_attention(proj, cum, *, batch, seq, blk):
    m = proj.shape[0]
    h, dh, nh = FOX_HEADS, HEAD_DIM, FOX_HEADS_PER_STEP
    assert dh == LANES and seq % blk == 0 and h % nh == 0
    nblk = seq // blk
    w = nh * dh
    kern = functools.partial(_fox_attn_kernel, blk=blk, nh=nh)
    return pl.pallas_call(
        kern, grid=(batch, h // nh, nblk),
        in_specs=[
            pl.BlockSpec((blk, w), lambda b, hh, i: (b * nblk + i, hh)),
            pl.BlockSpec((seq, w), lambda b, hh, i: (b, (h // nh) + hh)),
            pl.BlockSpec((seq, w), lambda b, hh, i: (b, 2 * (h // nh) + hh)),
            pl.BlockSpec((None, nh, 1, seq), lambda b, hh, i: (b, hh, 0, 0)),
        ],
        out_specs=pl.BlockSpec((blk, w), lambda b, hh, i: (b * nblk + i, hh)),
        out_shape=jax.ShapeDtypeStruct((m, h * dh), BF16),
        scratch_shapes=[pltpu.VMEM((nh, blk, LANES), F32), pltpu.VMEM((nh, blk, dh + LANES), F32),
                        pltpu.VMEM((nh, 1, 1), F32)],
        compiler_params=_params(("parallel", "parallel", "arbitrary")),
        name="fox_attention",
    )(proj, proj, proj, cum)


SB_LOGIT_CAP = 126.0
SB_DEAD_CARRY = SB_LOGIT_CAP + 150.0


SB_HEADS_PER_STEP = 4


def _sb_attn_kernel(q_ref, k_ref, v_ref, u_ref, o_ref, carry_scr, acc_scr, *, blk, sub, nh):
    qi = pl.program_id(2)
    q0 = pl.multiple_of(qi * blk, blk)
    nsub = blk // sub
    dh = HEAD_DIM

    def block(h, kstart, strict):
        hs = slice(h * dh, (h + 1) * dh)
        z = lax.dot_general(q_ref[:, hs], k_ref[pl.ds(kstart, blk), hs], (((1,), (1,)), ((), ())),
                            preferred_element_type=F32)
        z = jnp.minimum(z, SB_LOGIT_CAP)
        sp = jnp.log(1.0 + jnp.exp2(z)) * LOG2E
        if strict is not None:
            sp = jnp.where(strict, sp, 0.0)
        carry = carry_scr[h]
        parts = [None] * nsub
        for c in range(nsub - 1, -1, -1):
            sl = slice(c * sub, (c + 1) * sub)
            sp_c = sp[:, sl]
            g = jnp.dot(sp_c.astype(BF16), u_ref[...], preferred_element_type=F32)
            a_c = jnp.exp2(z[:, sl] - g - _rep(carry, sub // LANES))
            if strict is not None:
                a_c = jnp.where(strict[:, sl], a_c, 0.0)
            parts[c] = a_c.astype(BF16)
            carry = carry + jnp.sum(sp_c, axis=-1, keepdims=True)
        carry_scr[h] = carry
        acc_scr[:, hs] += jnp.dot(jnp.concatenate(parts, axis=1), v_ref[pl.ds(kstart, blk), hs],
                                  preferred_element_type=F32)

    carry_scr[...] = jnp.zeros_like(carry_scr)
    acc_scr[...] = jnp.zeros_like(acc_scr)
    row = lax.broadcasted_iota(jnp.int32, (blk, blk), 0)
    col = lax.broadcasted_iota(jnp.int32, (blk, blk), 1)
    for h in range(nh):
        block(h, q0, col < row)

    def alive():
        return jnp.min(carry_scr[...]) < SB_DEAD_CARRY

    def body(state):
        n, _ = state
        kstart = pl.multiple_of((qi - n) * blk, blk)
        for h in range(nh):
            block(h, kstart, None)
        return n + 1, alive()

    lax.while_loop(lambda st: jnp.logical_and(st[0] <= qi, st[1]), body, (jnp.int32(1), alive()))
    o_ref[...] = acc_scr[...].astype(o_ref.dtype)


def sb_attention(proj, *, batch, seq, blk, sub):
    m = proj.shape[0]
    h, dh, nh = SB_HEADS, HEAD_DIM, SB_HEADS_PER_STEP
    assert seq % blk == 0 and blk % sub == 0 and h % nh == 0
    nblk = seq // blk
    w = nh * dh
    u = jnp.asarray(np.tril(np.ones((sub, sub), np.float32)), BF16)
    kern = functools.partial(_sb_attn_kernel, blk=blk, sub=sub, nh=nh)
    return pl.pallas_call(
        kern, grid=(batch, h // nh, nblk),
        in_specs=[
            pl.BlockSpec((blk, w), lambda b, hh, i: (b * nblk + i, hh)),
            pl.BlockSpec((seq, w), lambda b, hh, i: (b, (h // nh) + hh)),
            pl.BlockSpec((seq, w), lambda b, hh, i: (b, 2 * (h // nh) + hh)),
            pl.BlockSpec((sub, sub), lambda b, hh, i: (0, 0)),
        ],
        out_specs=pl.BlockSpec((blk, w), lambda b, hh, i: (b * nblk + i, hh)),
        out_shape=jax.ShapeDtypeStruct((m, h * dh), BF16),
        scratch_shapes=[pltpu.VMEM((nh, blk, LANES), F32), pltpu.VMEM((blk, w), F32)],
        compiler_params=_params(("parallel", "parallel", "arbitrary")),
        name="sb_attention",
    )(proj, proj, proj, u)


LRU_TN = 3 * LANES
LRU_KW = 7 * LANES
LRU_HALO = 8


def _lru_windows(width, bd):
    starts = []
    for j in range(width // LRU_TN):
        c0, c1 = j * LRU_TN, (j + 1) * LRU_TN - 1
        r0, r1 = (c0 // bd) * bd, (c1 // bd + 1) * bd
        k0 = min((r0 // LANES) * LANES, width - LRU_KW)
        assert k0 <= r0 and r1 <= k0 + LRU_KW
        starts.append(k0)
    return starts


def _pack_lru_gates(w_r, w_i, width):
    nb, bd, _ = w_r.shape
    starts = _lru_windows(width, bd)
    dense_r = jax.scipy.linalg.block_diag(*w_r.astype(BF16))
    dense_i = jax.scipy.linalg.block_diag(*w_i.astype(BF16))
    tiles = []
    for j, k0 in enumerate(starts):
        cs = slice(j * LRU_TN, (j + 1) * LRU_TN)
        tiles.append(jnp.concatenate([dense_r[k0:k0 + LRU_KW, cs], dense_i[k0:k0 + LRU_KW, cs]], axis=1))
    return jnp.stack(tiles).astype(BF16), starts


def _lru_kernel(gate_ref, u_ref, cw_ref, cb_ref, wri_ref, br_ref, bi_ref, lam_ref, y_ref,
                ubuf, a_scr, b_scr, h_scr, *, starts, t):
    sblk = pl.program_id(1)
    width = u_ref.shape[1]

    @pl.when(sblk == 0)
    def _():
        ubuf[0:LRU_HALO, :] = jnp.zeros((LRU_HALO, width), F32)
        h_scr[...] = jnp.zeros_like(h_scr)

    ubuf[LRU_HALO:LRU_HALO + t, :] = u_ref[...]
    nconv = cw_ref.shape[0]
    uc = cb_ref[...] + cw_ref[nconv - 1:nconv, :] * u_ref[...]
    for j in range(nconv - 1):
        off = LRU_HALO - (nconv - 1) + j
        uc = uc + cw_ref[j:j + 1, :] * ubuf[off:off + t, :]
    ubuf[0:LRU_HALO, :] = u_ref[t - LRU_HALO:t, :]

    ucb = uc.astype(BF16)
    log_sig_lam = _log_sigmoid(lam_ref[...])
    row_in_group = lax.broadcasted_iota(jnp.int32, (1, SUBLANES, LRU_TN), 1)
    for j, k0 in enumerate(starts):
        cs = slice(j * LRU_TN, (j + 1) * LRU_TN)
        pre = jnp.dot(ucb[:, k0:k0 + LRU_KW], wri_ref[j], preferred_element_type=F32)
        r = jax.nn.sigmoid(pre[:, :LRU_TN] + br_ref[:, cs])
        ig = jax.nn.sigmoid(pre[:, LRU_TN:] + bi_ref[:, cs])
        log_a = LRU_C * r * log_sig_lam[:, cs]
        a = jnp.exp(log_a)
        bterm = jnp.sqrt(1.0 - a * a) * (ig * uc[:, cs])
        a = a.reshape(t // SUBLANES, SUBLANES, LRU_TN)
        bterm = bterm.reshape(t // SUBLANES, SUBLANES, LRU_TN)
        d = 1
        while d < SUBLANES:
            take = row_in_group >= d
            b_sh = pltpu.roll(bterm, d, 1)
            a_sh = pltpu.roll(a, d, 1)
            bterm = jnp.where(take, a * b_sh + bterm, bterm)
            a = jnp.where(take, a * a_sh, a)
            d *= 2
        a_scr[:, cs] = a.reshape(t, LRU_TN)
        b_scr[:, cs] = bterm.reshape(t, LRU_TN)

    h_in = h_scr[...]
    for r in range(t // SUBLANES):
        rows = slice(r * SUBLANES, (r + 1) * SUBLANES)
        h = b_scr[rows, :] + a_scr[rows, :] * h_in
        b_scr[rows, :] = h
        h_in = h[SUBLANES - 1:SUBLANES, :]
    h_scr[...] = h_in
    y_ref[...] = (b_scr[...] * jax.nn.gelu(gate_ref[...], approximate=True)).astype(y_ref.dtype)


def lru_core(proj, conv_w, conv_b, wri, starts, b_r, b_i, lam, *, batch, seq, t):
    m = proj.shape[0]
    width = proj.shape[1] // 2
    nt = seq // t
    assert seq % t == 0 and t % SUBLANES == 0
    vec = lambda: pl.BlockSpec((1, width), lambda b, s: (0, 0))
    kern = functools.partial(_lru_kernel, starts=tuple(starts), t=t)
    return pl.pallas_call(
        kern, grid=(batch, nt),
        in_specs=[pl.BlockSpec((t, width), lambda b, s: (b * nt + s, 0)),
                  pl.BlockSpec((t, width), lambda b, s: (b * nt + s, 1)),
                  pl.BlockSpec(conv_w.shape, lambda b, s: (0, 0)),
                  vec(),
                  pl.BlockSpec(wri.shape, lambda b, s: (0, 0, 0)),
                  vec(), vec(), vec()],
        out_specs=pl.BlockSpec((t, width), lambda b, s: (b * nt + s, 0)),
        out_shape=jax.ShapeDtypeStruct((m, width), BF16),
        scratch_shapes=[pltpu.VMEM((t + LRU_HALO, width), F32),
                        pltpu.VMEM((t, width), F32),
                        pltpu.VMEM((t, width), F32),
                        pltpu.VMEM((1, width), F32)],
        compiler_params=_params(("parallel", "arbitrary")),
        name="lru_core",
    )(proj, proj, conv_w, conv_b.reshape(1, width), wri, b_r.reshape(1, width),
      b_i.reshape(1, width), lam.reshape(1, width))


def _col_to_row(col, eye):
    return jnp.sum(jnp.where(eye, col, 0.0), axis=0, keepdims=True)


def _mlstm_kernel(q_ref, k_ref, v_ref, o_ref, g_ref, bif_ref, hg_ref, y_ref,
                  c_scr, n_scr, m_scr, *, nheads, dk, dv):
    chunk = pl.program_id(1)
    length = q_ref.shape[0]

    @pl.when(chunk == 0)
    def _():
        c_scr[...] = jnp.zeros_like(c_scr)
        n_scr[...] = jnp.zeros_like(n_scr)
        m_scr[...] = jnp.full_like(m_scr, MLSTM_M_INIT)

    row = lax.broadcasted_iota(jnp.int32, (length, length), 0)
    col = lax.broadcasted_iota(jnp.int32, (length, length), 1)
    eye = row == col
    causal = col <= row
    g = g_ref[...] + bif_ref[...]
    lane = lax.broadcasted_iota(jnp.int32, g.shape, 1)
    kscale = dk ** -0.5

    for head in range(nheads):
        ks_ = slice(head * dk, (head + 1) * dk)
        vs_ = slice(head * dv, (head + 1) * dv)
        i_col = jnp.sum(jnp.where(lane == head, g, 0.0), axis=-1, keepdims=True)
        f_col = jnp.sum(jnp.where(lane == nheads + head, g, 0.0), axis=-1, keepdims=True)
        lf_col = _log_sigmoid(f_col)
        lf_row = _col_to_row(lf_col, eye)
        i_row = _col_to_row(i_col, eye)
        b_col = jnp.sum(jnp.where(causal, lf_row, 0.0), axis=-1, keepdims=True)
        b_row = _col_to_row(b_col, eye)
        gsum = jnp.sum(lf_row, axis=-1, keepdims=True)

        m_prev = m_scr[head]
        dm = jnp.where(causal, b_col - b_row + i_row, NEG)
        inter = b_col + m_prev
        m_t = jnp.maximum(inter, jnp.max(dm, axis=-1, keepdims=True))
        w_intra = jnp.exp(dm - m_t)
        w_inter = jnp.exp(inter - m_t)

        q = q_ref[:, ks_]
        ks = k_ref[:, ks_] * kscale
        v = v_ref[:, vs_]
        s = lax.dot_general(q, ks, (((1,), (1,)), ((), ())), preferred_element_type=F32) * w_intra
        num = (w_inter * jnp.dot(q, c_scr[head].astype(BF16), preferred_element_type=F32)
               + jnp.dot(s.astype(BF16), v, preferred_element_type=F32))
        qn = jnp.sum(q.astype(F32) * n_scr[head], axis=-1, keepdims=True)
        den = w_inter * qn + jnp.sum(s, axis=-1, keepdims=True)
        hval = num / jnp.maximum(jnp.abs(den), jnp.exp(-m_t))

        hn = hval * _rms_scale(hval) * hg_ref[:, vs_]
        y_ref[:, vs_] = (jax.nn.sigmoid(o_ref[:, vs_].astype(F32)) * hn).astype(y_ref.dtype)

        wk_col = gsum - b_col + i_col
        m_new = jnp.maximum(gsum + m_prev, jnp.max(wk_col, axis=0, keepdims=True))
        decay = jnp.exp(gsum + m_prev - m_new)
        kw = jnp.exp(wk_col - m_new) * ks.astype(F32)
        c_scr[head] = decay * c_scr[head] + lax.dot_general(
            kw.astype(BF16), v, (((0,), (0,)), ((), ())), preferred_element_type=F32)
        n_scr[head] = decay * n_scr[head] + jnp.sum(kw, axis=0, keepdims=True)
        m_scr[head] = m_new


def mlstm_core(proj, gates, b_if, head_g, *, batch, seq, chunk):
    m = proj.shape[0]
    d = head_g.shape[0]
    nh = MLSTM_HEADS
    dk, dv = d // 2 // nh, d // nh
    qk = nh * dk
    nc = seq // chunk
    ng = gates.shape[1]
    bif = jnp.zeros((1, ng), F32).at[0, :2 * nh].set(b_if.reshape(-1))
    kern = functools.partial(_mlstm_kernel, nheads=nh, dk=dk, dv=dv)
    return pl.pallas_call(
        kern, grid=(batch, nc),
        in_specs=[pl.BlockSpec((chunk, qk), lambda b, c: (b * nc + c, 0)),
                  pl.BlockSpec((chunk, qk), lambda b, c: (b * nc + c, 1)),
                  pl.BlockSpec((chunk, d), lambda b, c: (b * nc + c, (2 * qk) // d)),
                  pl.BlockSpec((chunk, d), lambda b, c: (b * nc + c, (2 * qk) // d + 1)),
                  pl.BlockSpec((chunk, ng), lambda b, c: (b * nc + c, 0)),
                  pl.BlockSpec((1, ng), lambda b, c: (0, 0)),
                  pl.BlockSpec((1, d), lambda b, c: (0, 0))],
        out_specs=pl.BlockSpec((chunk, d), lambda b, c: (b * nc + c, 0)),
        out_shape=jax.ShapeDtypeStruct((m, d), BF16),
        scratch_shapes=[pltpu.VMEM((nh, dk, dv), F32), pltpu.VMEM((nh, 1, dk), F32),
                        pltpu.VMEM((nh, 1, 1), F32)],
        compiler_params=_params(("parallel", "arbitrary")),
        name="mlstm_core",
    )(proj, proj, proj, proj, gates, bif, head_g.reshape(1, d))


def _pad_cols(w, n):
    return jnp.pad(w, ((0, 0), (0, n - w.shape[1])))


def _tiles(seq):
    return dict(
        proj_rows=1024,
        proj_cols=2048,
        lru_proj_rows=512,
        out_rows=512,
        mlp_rows=512,
        mlp_cols=1024,
        attn_blk=min(512, seq),
        sb_sub=256,
        lru_rows=min(256, seq),
        mlstm_chunk=min(256, seq),
    )


def kernel(x, norm_g, mlp_w1, mlp_w2, fox_w_in, fox_b_f, fox_w_out, lru_w_in, lru_conv_w, lru_conv_b,
           lru_w_r, lru_b_r, lru_w_i, lru_b_i, lru_lambda, lru_w_out, sb_w_in, sb_w_out,
           mlstm_w_in, mlstm_b_if, mlstm_head_g, mlstm_w_out):
    batch, seq, d = x.shape
    depth = norm_g.shape[0]
    n_mixers = 4
    tl = _tiles(seq)
    xf = x.reshape(batch * seq, d)
    q_scale = HEAD_DIM ** -0.5 * LOG2E
    proj_tiles = dict(tm=tl["proj_rows"], tn=tl["proj_cols"])
    w1_all, w2_all = mlp_w1.astype(BF16), mlp_w2.astype(BF16)

    for i in range(depth):
        mixer, j = i % n_mixers, i // n_mixers
        if mixer == 0:
            w = fox_w_in[j]
            proj, fgate = norm_proj(xf, norm_g[i, 0], w.astype(BF16),
                                    _pad_cols(w[:, 3 * d:], LANES).astype(BF16), n_cols=3 * d,
                                    out_dtype=BF16, q_scale=q_scale, q_cols=d, **proj_tiles)
            f_t = fgate[:, :FOX_HEADS].reshape(batch, seq, FOX_HEADS).transpose(0, 2, 1)
            cum = fox_cum(f_t, fox_b_f[j]).reshape(batch, FOX_HEADS, 1, seq)
            mixed = fox_attention(proj, cum, batch=batch, seq=seq, blk=tl["attn_blk"])
            w_out = fox_w_out[j]
        elif mixer == 1:
            width = lru_w_in.shape[2] // 2
            proj = norm_proj(xf, norm_g[i, 0], lru_w_in[j].astype(BF16), out_dtype=F32,
                             tm=tl["lru_proj_rows"], tn=width)
            wri, starts = _pack_lru_gates(lru_w_r[j], lru_w_i[j], width)
            mixed = lru_core(proj, lru_conv_w[j], lru_conv_b[j], wri, starts, lru_b_r[j], lru_b_i[j],
                             lru_lambda[j], batch=batch, seq=seq, t=tl["lru_rows"])
            w_out = lru_w_out[j]
        elif mixer == 2:
            proj = norm_proj(xf, norm_g[i, 0], sb_w_in[j].astype(BF16), out_dtype=BF16,
                             q_scale=q_scale, q_cols=d, **proj_tiles)
            mixed = sb_attention(proj, batch=batch, seq=seq, blk=tl["attn_blk"], sub=tl["sb_sub"])
            w_out = sb_w_out[j]
        else:
            w = mlstm_w_in[j]
            proj, gates = norm_proj(xf, norm_g[i, 0], w.astype(BF16),
                                    _pad_cols(w[:, 3 * d:], LANES).astype(BF16), n_cols=3 * d,
                                    out_dtype=BF16, **proj_tiles)
            mixed = mlstm_core(proj, gates, mlstm_b_if[j], mlstm_head_g[j],
                               batch=batch, seq=seq, chunk=tl["mlstm_chunk"])
            w_out = mlstm_w_out[j]
        xf = proj_norm_res(mixed, w_out.astype(BF16), norm_g[i, 1], xf, tm=tl["out_rows"])
        xf = mlp_block(xf, norm_g[i, 2], w1_all, w2_all, norm_g[i, 3], layer=i,
                       tm=tl["mlp_rows"], tf=tl["mlp_cols"])
    return xf.reshape(batch, seq, d)
```
